```python
import math
import jax, jax.numpy as jnp
from jax import lax
import numpy as np

D_MODEL = 1024
BATCH = 2
SEQ = 8192
DEPTH = 2

CHUNK = 64
Q_BLOCK = 128
DIFF_HEADS = 4
DIFF_QK_DIM = 64
DIFF_V_DIM = 2 * DIFF_QK_DIM
DIFF_QK_WIDTH = DIFF_HEADS * 2 * DIFF_QK_DIM
DIFF_WIDTH = DIFF_HEADS * DIFF_V_DIM
SSM_WIDTH = D_MODEL - DIFF_WIDTH
SSM_GROUP = 16
SSM_GROUPS = SSM_WIDTH // SSM_GROUP
SSM_STATE = 64
EVEN_IN = 2 * DIFF_QK_WIDTH + DIFF_WIDTH + SSM_WIDTH
FOX_HEAD_DIM = 64
FOX_HEADS = D_MODEL // FOX_HEAD_DIM
FOX_IN = 3 * D_MODEL + FOX_HEADS
D_FF = 4 * D_MODEL
ROPE_THETA = 10000.0
LN_EPS = 1e-5
RMS_EPS = 1e-6
DEEPNORM_ALPHA = (2 * DEPTH) ** 0.25
DEEPNORM_BETA = (8 * DEPTH) ** -0.25
N_EVEN = (DEPTH + 1) // 2
N_ODD = DEPTH // 2

kernel_name = "hybrid_diffattn_s5_fox_deepnorm"


def layer_norm(x, g, b):
    xf = x.astype(jnp.float32)
    mu = jnp.mean(xf, axis=-1, keepdims=True)
    var = jnp.mean(jnp.square(xf - mu), axis=-1, keepdims=True)
    return ((xf - mu) * lax.rsqrt(var + LN_EPS) * g + b).astype(x.dtype)


def rope_tables(seq, dim):
    inv = ROPE_THETA ** (-jnp.arange(0, dim, 2, dtype=jnp.float32) / dim)
    ang = jnp.arange(seq, dtype=jnp.float32)[:, None] * inv[None, :]
    return jnp.cos(ang), jnp.sin(ang)


def apply_rope(x, cos, sin):
    x1, x2 = jnp.split(x, 2, axis=-1)
    c = cos[None, :, None, :].astype(x.dtype)
    s = sin[None, :, None, :].astype(x.dtype)
    return jnp.concatenate([x1 * c - x2 * s, x1 * s + x2 * c], axis=-1)


def diff_attention(q, k, v, lam):
    bsz, h, _, s, d = q.shape
    nb = s // Q_BLOCK
    scale = d ** -0.5
    k_chunk = jnp.arange(s) // CHUNK

    def block(i):
        q0 = i * Q_BLOCK
        qb = lax.dynamic_slice_in_dim(q, q0, Q_BLOCK, axis=3)
        logits = jnp.einsum('bhcqd,bhckd->bhcqk', qb, k).astype(jnp.float32) * scale
        q_chunk = (q0 + jnp.arange(Q_BLOCK)) // CHUNK
        mask = k_chunk[None, :] <= q_chunk[:, None]
        p = jax.nn.softmax(jnp.where(mask, logits, -jnp.inf), axis=-1)
        p_diff = p[:, :, 0] - lam * p[:, :, 1]
        return jnp.einsum('bhqk,bhkv->bhqv', p_diff.astype(v.dtype), v)

    out = lax.map(block, jnp.arange(nb))
    return out.transpose(1, 0, 3, 2, 4).reshape(bsz, s, h, v.shape[-1])


def s5_mixer(u, a_re, a_im, log_dt, b_re, b_im, c_re, c_im, d_skip, glu_w, glu_b):
    bsz, s, _ = u.shape
    f32 = jnp.float32
    uf = u.astype(f32)
    ug = uf.reshape(bsz, s, SSM_GROUPS, SSM_GROUP)
    a_re = a_re.astype(f32)
    a_im = a_im.astype(f32)
    dt = jnp.exp(log_dt.astype(f32))[:, None]
    mag = jnp.exp(a_re * dt)
    ab_re = mag * jnp.cos(a_im * dt)
    ab_im = mag * jnp.sin(a_im * dt)
    den = jnp.square(a_re) + jnp.square(a_im)
    nr = ab_re - 1.0
    ni = ab_im
    g_re = (nr * a_re + ni * a_im) / den
    g_im = (ni * a_re - nr * a_im) / den
    b_re = b_re.astype(f32)
    b_im = b_im.astype(f32)
    bb_re = g_re[..., None] * b_re - g_im[..., None] * b_im
    bb_im = g_re[..., None] * b_im + g_im[..., None] * b_re
    bu_re = jnp.einsum('bsgh,gph->bsgp', ug, bb_re)
    bu_im = jnp.einsum('bsgh,gph->bsgp', ug, bb_im)
    a_seq_re = jnp.broadcast_to(ab_re[None, None], (1, s, SSM_GROUPS, SSM_STATE))
    a_seq_im = jnp.broadcast_to(ab_im[None, None], (1, s, SSM_GROUPS, SSM_STATE))

    def combine(left, right):
        a1r, a1i, b1r, b1i = left
        a2r, a2i, b2r, b2i = right
        return (a2r * a1r - a2i * a1i,
                a2r * a1i + a2i * a1r,
                a2r * b1r - a2i * b1i + b2r,
                a2r * b1i + a2i * b1r + b2i)

    _, _, x_re, x_im = lax.associative_scan(combine, (a_seq_re, a_seq_im, bu_re, bu_im), axis=1)
    y = (jnp.einsum('bsgp,ghp->bsgh', x_re, c_re.astype(f32))
         - jnp.einsum('bsgp,ghp->bsgh', x_im, c_im.astype(f32)))
    y = y.reshape(bsz, s, SSM_WIDTH) + d_skip.astype(f32) * uf
    y = jax.nn.gelu(y)
    y = y * jax.nn.sigmoid(y @ glu_w.astype(f32) + glu_b.astype(f32))
    return y.astype(u.dtype)


def even_mixer(h, w_in, lam_vecs, subln_g, a_re, a_im, log_dt, b_re, b_im, c_re, c_im,
               d_skip, glu_w, glu_b, w_out, lam_init, cos, sin):
    bsz, s, _ = h.shape
    proj = h @ w_in
    q, k, v, u = jnp.split(proj, [DIFF_QK_WIDTH, 2 * DIFF_QK_WIDTH,
                                  2 * DIFF_QK_WIDTH + DIFF_WIDTH], axis=-1)
    q = apply_rope(q.reshape(bsz, s, 2 * DIFF_HEADS, DIFF_QK_DIM), cos, sin)
    k = apply_rope(k.reshape(bsz, s, 2 * DIFF_HEADS, DIFF_QK_DIM), cos, sin)
    q = q.reshape(bsz, s, DIFF_HEADS, 2, DIFF_QK_DIM).transpose(0, 2, 3, 1, 4)
    k = k.reshape(bsz, s, DIFF_HEADS, 2, DIFF_QK_DIM).transpose(0, 2, 3, 1, 4)
    v = v.reshape(bsz, s, DIFF_HEADS, DIFF_V_DIM).transpose(0, 2, 1, 3)
    lv = lam_vecs.astype(jnp.float32)
    lam = jnp.exp(jnp.sum(lv[0] * lv[1])) - jnp.exp(jnp.sum(lv[2] * lv[3])) + lam_init
    attn = diff_attention(q, k, v, lam).astype(jnp.float32)
    attn = attn * lax.rsqrt(jnp.mean(jnp.square(attn), axis=-1, keepdims=True) + RMS_EPS)
    attn = (attn * subln_g.astype(jnp.float32) * (1.0 - lam_init)).astype(h.dtype)
    y_ssm = s5_mixer(u, a_re, a_im, log_dt, b_re, b_im, c_re, c_im, d_skip, glu_w, glu_b)
    mixed = jnp.concatenate([attn.reshape(bsz, s, DIFF_WIDTH), y_ssm], axis=-1)
    return mixed @ w_out


def fox_attention(q, k, v, fcum):
    bsz, h, s, d = q.shape
    nb = s // Q_BLOCK
    scale = d ** -0.5
    k_pos = jnp.arange(s)

    def block(i):
        q0 = i * Q_BLOCK
        qb = lax.dynamic_slice_in_dim(q, q0, Q_BLOCK, axis=2)
        fq = lax.dynamic_slice_in_dim(fcum, q0, Q_BLOCK, axis=2)
        logits = (jnp.einsum('bhqd,bhkd->bhqk', qb, k).astype(jnp.float32) * scale
                  + fq[..., None] - fcum[:, :, None, :])
        q_pos = q0 + jnp.arange(Q_BLOCK)
        mask = k_pos[None, :] <= q_pos[:, None]
        p = jax.nn.softmax(jnp.where(mask, logits, -jnp.inf), axis=-1)
        return jnp.einsum('bhqk,bhkd->bhqd', p.astype(v.dtype), v)

    out = lax.map(block, jnp.arange(nb))
    return out.transpose(1, 0, 3, 2, 4).reshape(bsz, s, h * d)


def odd_mixer(h, w_in, b_f, w_out):
    bsz, s, _ = h.shape
    proj = h @ w_in
    q, k, v, f_logit = jnp.split(proj, [D_MODEL, 2 * D_MODEL, 3 * D_MODEL], axis=-1)
    q = q.reshape(bsz, s, FOX_HEADS, FOX_HEAD_DIM).transpose(0, 2, 1, 3)
    k = k.reshape(bsz, s, FOX_HEADS, FOX_HEAD_DIM).transpose(0, 2, 1, 3)
    v = v.reshape(bsz, s, FOX_HEADS, FOX_HEAD_DIM).transpose(0, 2, 1, 3)
    log_f = jax.nn.log_sigmoid(f_logit.astype(jnp.float32) + b_f.astype(jnp.float32))
    fcum = jnp.cumsum(log_f, axis=1).transpose(0, 2, 1)
    return fox_attention(q, k, v, fcum) @ w_out


def squared_relu_mlp(h, w1, w2):
    return jnp.square(jax.nn.relu(h @ w1)) @ w2


def setup_inputs(seed: int = 0) -> dict:
    key = jax.random.key(seed)
    ks = iter(jax.random.split(key, 32))
    f32 = jnp.float32
    D = D_MODEL
    E, O, L = N_EVEN, N_ODD, DEPTH

    def nrm(shape, scale):
        return jax.random.normal(next(ks), shape, f32) * scale

    x = nrm((BATCH, SEQ, D), 1.0)
    ab_w_in = nrm((E, D, EVEN_IN), D ** -0.5)
    ab_lambda = nrm((E, 4, DIFF_QK_DIM), 0.1)
    ab_subln_g = 1.0 + nrm((E, DIFF_HEADS, DIFF_V_DIM), 0.02)
    n_idx = jnp.arange(SSM_STATE, dtype=f32)
    ssm_a_re = -0.5 + nrm((E, SSM_GROUPS, SSM_STATE), 0.01)
    ssm_a_im = math.pi * n_idx + nrm((E, SSM_GROUPS, SSM_STATE), 0.01)
    ssm_log_dt = jax.random.uniform(next(ks), (E, SSM_GROUPS), f32,
                                    math.log(1e-3), math.log(1e-1))
    ssm_b_re = nrm((E, SSM_GROUPS, SSM_STATE, SSM_GROUP), (2 * SSM_GROUP) ** -0.5)
    ssm_b_im = nrm((E, SSM_GROUPS, SSM_STATE, SSM_GROUP), (2 * SSM_GROUP) ** -0.5)
    ssm_c_re = nrm((E, SSM_GROUPS, SSM_GROUP, SSM_STATE), SSM_STATE ** -0.5)
    ssm_c_im = nrm((E, SSM_GROUPS, SSM_GROUP, SSM_STATE), SSM_STATE ** -0.5)
    ssm_d = nrm((E, SSM_WIDTH), 1.0)
    ssm_glu_w = nrm((E, SSM_WIDTH, SSM_WIDTH), SSM_WIDTH ** -0.5)
    ssm_glu_b = nrm((E, SSM_WIDTH), 0.01)
    ab_w_out = nrm((E, D, D), DEEPNORM_BETA * D ** -0.5)
    fox_w_in = nrm((O, D, FOX_IN), D ** -0.5)
    fox_b_f = jax.random.uniform(next(ks), (O, FOX_HEADS), f32, 1.0, 5.0)
    fox_w_out = nrm((O, D, D), DEEPNORM_BETA * D ** -0.5)
    ln1_g = 1.0 + nrm((L, D), 0.02)
    ln1_b = nrm((L, D), 0.02)
    ffn_w1 = nrm((L, D, D_FF), D ** -0.5)
    ffn_w2 = nrm((L, D_FF, D), DEEPNORM_BETA * D_FF ** -0.5)
    ln2_g = 1.0 + nrm((L, D), 0.02)
    ln2_b = nrm((L, D), 0.02)
    return {"x": x, "ab_w_in": ab_w_in, "ab_lambda": ab_lambda, "ab_subln_g": ab_subln_g,
            "ssm_a_re": ssm_a_re, "ssm_a_im": ssm_a_im, "ssm_log_dt": ssm_log_dt,
            "ssm_b_re": ssm_b_re, "ssm_b_im": ssm_b_im, "ssm_c_re": ssm_c_re,
            "ssm_c_im": ssm_c_im, "ssm_d": ssm_d, "ssm_glu_w": ssm_glu_w,
            "ssm_glu_b": ssm_glu_b, "ab_w_out": ab_w_out, "fox_w_in": fox_w_in,
            "fox_b_f": fox_b_f, "fox_w_out": fox_w_out, "ln1_g": ln1_g, "ln1_b": ln1_b,
            "ffn_w1": ffn_w1, "ffn_w2": ffn_w2, "ln2_g": ln2_g, "ln2_b": ln2_b}


def reference(x, ab_w_in, ab_lambda, ab_subln_g, ssm_a_re, ssm_a_im, ssm_log_dt,
              ssm_b_re, ssm_b_im, ssm_c_re, ssm_c_im, ssm_d, ssm_glu_w, ssm_glu_b,
              ab_w_out, fox_w_in, fox_b_f, fox_w_out, ln1_g, ln1_b, ffn_w1, ffn_w2,
              ln2_g, ln2_b):
    seq = x.shape[1]
    cos, sin = rope_tables(seq, DIFF_QK_DIM)
    h = x
    for l in range(DEPTH):
        if l % 2 == 0:
            i = l // 2
            lam_init = 0.8 - 0.6 * math.exp(-0.3 * l)
            mix = even_mixer(h, ab_w_in[i], ab_lambda[i], ab_subln_g[i], ssm_a_re[i],
                             ssm_a_im[i], ssm_log_dt[i], ssm_b_re[i], ssm_b_im[i],
                             ssm_c_re[i], ssm_c_im[i], ssm_d[i], ssm_glu_w[i],
                             ssm_glu_b[i], ab_w_out[i], lam_init, cos, sin)
        else:
            i = l // 2
            mix = odd_mixer(h, fox_w_in[i], fox_b_f[i], fox_w_out[i])
        h = layer_norm(DEEPNORM_ALPHA * h + mix, ln1_g[l], ln1_b[l])
        h = layer_norm(DEEPNORM_ALPHA * h + squared_relu_mlp(h, ffn_w1[l], ffn_w2[l]),
                       ln2_g[l], ln2_b[l])
    return h
```

```python
import functools
import math

import jax
import jax.numpy as jnp
from jax import lax
from jax.experimental import pallas as pl
from jax.experimental.pallas import tpu as pltpu

F32 = jnp.float32
BF16 = jnp.bfloat16

LANES = 128
SUBLANES = 8
VMEM_LIMIT = 48 * 1024 * 1024

CHUNK = 64
QK_DIM = 64
ROPE_THETA = 10000.0
LN_EPS = 1e-5
RMS_EPS = 1e-6
SSM_GROUP = 16
SSM_STATE = 64
SSM_SEG = 64
SSM_ROWS = SUBLANES * SSM_SEG
SSM_COLS = 512
ATT_TILE = 256
ROW_TILE = 512
FFN_ROW_TILE = 1024
FFN_COL_TILE = 512
GATE_TILE = 256


def _params(*sem):
    return pltpu.CompilerParams(dimension_semantics=sem, vmem_limit_bytes=VMEM_LIMIT)


def _layer_norm(y, g, b):
    mu = jnp.mean(y, axis=-1, keepdims=True)
    d = y - mu
    var = jnp.mean(d * d, axis=-1, keepdims=True)
    return d * lax.rsqrt(var + LN_EPS) * g + b


def _proj_even_kernel(x_ref, w_ref, cos_ref, sin_ref, o_ref, *, qk_width, scale):
    xb = x_ref[...].astype(BF16)
    cosv = cos_ref[...]
    sinv = sin_ref[...]
    lane = lax.broadcasted_iota(jnp.int32, cosv.shape, 1)
    first_half = (lane % QK_DIM) < (QK_DIM // 2)
    n_out = o_ref.shape[1]
    for c0 in range(0, n_out, 512):
        acc = jnp.dot(xb, w_ref[:, c0:c0 + 512], preferred_element_type=F32)
        for l0 in range(0, 512, LANES):
            a = acc[:, l0:l0 + LANES]
            col = c0 + l0
            if col < 2 * qk_width:
                partner = jnp.where(first_half,
                                    pltpu.roll(a, LANES - QK_DIM // 2, 1),
                                    pltpu.roll(a, QK_DIM // 2, 1))
                a = a * cosv + partner * sinv
                if col < qk_width:
                    a = a * scale
            o_ref[:, col:col + LANES] = a.astype(o_ref.dtype)


def _proj_even(x2, w, cos128, sin128, seq, qk_width):
    t, d = x2.shape
    n = w.shape[1]
    tm = ROW_TILE
    nseq = seq // tm
    return pl.pallas_call(
        functools.partial(_proj_even_kernel, qk_width=qk_width, scale=QK_DIM ** -0.5),
        grid=(t // tm,),
        in_specs=[pl.BlockSpec((tm, d), lambda i: (i, 0)),
                  pl.BlockSpec((d, n), lambda i: (0, 0)),
                  pl.BlockSpec((tm, LANES), lambda i: (i % nseq, 0)),
                  pl.BlockSpec((tm, LANES), lambda i: (i % nseq, 0))],
        out_specs=pl.BlockSpec((tm, n), lambda i: (i, 0)),
        out_shape=jax.ShapeDtypeStruct((t, n), BF16),
        compiler_params=_params("parallel"),
        name="proj_even",
    )(x2, w, cos128, sin128)


def _proj_odd_kernel(x_ref, w_ref, o_ref, f_ref, *, q_width, scale):
    xb = x_ref[...].astype(BF16)
    n_out = o_ref.shape[1]
    for c0 in range(0, n_out, 512):
        acc = jnp.dot(xb, w_ref[:, c0:c0 + 512], preferred_element_type=F32)
        if c0 < q_width:
            acc = acc * scale
        o_ref[:, c0:c0 + 512] = acc.astype(o_ref.dtype)
    f_ref[...] = jnp.dot(xb, w_ref[:, n_out:n_out + LANES], preferred_element_type=F32)


def _proj_odd(x2, w, q_width):
    t, d = x2.shape
    n = w.shape[1] - LANES
    tm = ROW_TILE
    return pl.pallas_call(
        functools.partial(_proj_odd_kernel, q_width=q_width, scale=QK_DIM ** -0.5),
        grid=(t // tm,),
        in_specs=[pl.BlockSpec((tm, d), lambda i: (i, 0)),
                  pl.BlockSpec((d, n + LANES), lambda i: (0, 0))],
        out_specs=[pl.BlockSpec((tm, n), lambda i: (i, 0)),
                   pl.BlockSpec((tm, LANES), lambda i: (i, 0))],
        out_shape=[jax.ShapeDtypeStruct((t, n), BF16),
                   jax.ShapeDtypeStruct((t, LANES), F32)],
        compiler_params=_params("parallel"),
        name="proj_odd",
    )(x2, w)


def _fgate_kernel(f_ref, b_ref, o_ref, carry_ref):
    @pl.when(pl.program_id(1) == 0)
    def _():
        carry_ref[...] = jnp.zeros_like(carry_ref)

    logf = jax.nn.log_sigmoid(f_ref[...] + b_ref[...])
    n = logf.shape[0]
    row = lax.broadcasted_iota(jnp.int32, (n, n), 0)
    col = lax.broadcasted_iota(jnp.int32, (n, n), 1)
    tri = (col <= row).astype(F32)
    cum = jnp.dot(tri, logf, preferred_element_type=F32,
                  precision=lax.Precision.HIGHEST) + carry_ref[0:1, :]
    o_ref[...] = cum
    carry_ref[...] = jnp.broadcast_to(cum[n - 1:n, :], carry_ref.shape)


def _fgate(flog, b_pad, batch, seq):
    tg = GATE_TILE
    nb = seq // tg
    return pl.pallas_call(
        _fgate_kernel,
        grid=(batch, nb),
        in_specs=[pl.BlockSpec((tg, LANES), lambda b, i: (b * nb + i, 0)),
                  pl.BlockSpec((1, LANES), lambda b, i: (0, 0))],
        out_specs=pl.BlockSpec((tg, LANES), lambda b, i: (b * nb + i, 0)),
        out_shape=jax.ShapeDtypeStruct(flog.shape, F32),
        scratch_shapes=[pltpu.VMEM((SUBLANES, LANES), F32)],
        compiler_params=_params("parallel", "arbitrary"),
        name="fgate_cumsum",
    )(flog, b_pad)


def _attn_kernel(*refs, mode, tile, lam_init):
    if mode == "diff":
        q_ref, k_ref, v_ref, lam_ref, g_ref, o_ref, qm_ref, m_ref, l_ref, acc_ref = refs
    else:
        q_ref, k_ref, v_ref, fq_ref, fk_ref, o_ref, qm_ref, m_ref, l_ref, acc_ref = refs
    qi = pl.program_id(2)
    q = q_ref[...]
    lane = lax.broadcasted_iota(jnp.int32, q.shape, 1)
    low = lane < QK_DIM
    zero = jnp.zeros_like(q)
    qm_ref[0] = jnp.where(low, q, zero)
    qm_ref[1] = jnp.where(low, zero, q)
    m_ref[...] = jnp.full(m_ref.shape, -jnp.inf, F32)
    l_ref[...] = jnp.zeros_like(l_ref)
    acc_ref[...] = jnp.zeros_like(acc_ref)

    if mode == "fox":
        pair = pl.program_id(1)
        fblk = fq_ref[...]
        fq = [jnp.sum(jnp.where(lane == 2 * pair + mp, fblk, 0.0), axis=1, keepdims=True)
              for mp in range(2)]

    row = lax.broadcasted_iota(jnp.int32, (tile, tile), 0)
    col = lax.broadcasted_iota(jnp.int32, (tile, tile), 1)
    if mode == "diff":
        diag_mask = (col // CHUNK) <= (row // CHUNK)
    else:
        diag_mask = col <= row

    def step(ki, masked):
        k0 = pl.multiple_of(ki * tile, tile)
        k = k_ref[pl.ds(k0, tile), :]
        v = v_ref[pl.ds(k0, tile), :]
        for mp in range(2):
            s = lax.dot_general(qm_ref[mp], k, (((1,), (1,)), ((), ())),
                                preferred_element_type=F32)
            if mode == "fox":
                s = s - fk_ref[ki, mp:mp + 1, :]
            if masked:
                s = jnp.where(diag_mask, s, -jnp.inf)
            m_old = m_ref[mp]
            row_max = jnp.max(s, axis=1, keepdims=True)
            if mode == "fox":
                m_new = jnp.maximum(m_old, row_max + fq[mp])
                p = jnp.exp(s + (fq[mp] - m_new))
            else:
                m_new = jnp.maximum(m_old, row_max)
                p = jnp.exp(s - m_new)
            alpha = jnp.exp(m_old - m_new)
            l_ref[mp] = alpha * l_ref[mp] + jnp.sum(p, axis=1, keepdims=True)
            acc_ref[mp] = alpha * acc_ref[mp] + jnp.dot(p.astype(BF16), v,
                                                        preferred_element_type=F32)
            m_ref[mp] = m_new

    def body(ki, carry):
        step(ki, False)
        return carry

    lax.fori_loop(0, qi, body, 0)
    step(qi, True)

    o0 = acc_ref[0] / l_ref[0]
    o1 = acc_ref[1] / l_ref[1]
    if mode == "diff":
        lv = lam_ref[...]
        lam = (jnp.exp(jnp.sum(lv[0:1] * lv[1:2], axis=1, keepdims=True))
               - jnp.exp(jnp.sum(lv[2:3] * lv[3:4], axis=1, keepdims=True)) + lam_init)
        d = o0 - lam * o1
        d = d * lax.rsqrt(jnp.mean(d * d, axis=-1, keepdims=True) + RMS_EPS)
        o_ref[...] = (d * g_ref[...] * (1.0 - lam_init)).astype(o_ref.dtype)
    else:
        o_ref[...] = jnp.where(low, o0, o1).astype(o_ref.dtype)


def _attention(mode, qkv, batch, seq, nblk, extra, lam_init=0.0):
    tile = ATT_TILE
    nq = seq // tile
    q_spec = pl.BlockSpec((tile, LANES), lambda b, p, i: (b * nq + i, p))
    k_spec = pl.BlockSpec((seq, LANES), lambda b, p, i: (b, nblk + p))
    v_spec = pl.BlockSpec((seq, LANES), lambda b, p, i: (b, 2 * nblk + p))
    if mode == "diff":
        lam_vecs, gains = extra
        extra_specs = [pl.BlockSpec(lam_vecs.shape, lambda b, p, i: (0, 0)),
                       pl.BlockSpec((1, LANES), lambda b, p, i: (0, p))]
    else:
        fcum, fcum_t = extra
        extra_specs = [pl.BlockSpec((tile, LANES), lambda b, p, i: (b * nq + i, 0)),
                       pl.BlockSpec((None, None, nq, 2, tile),
                                    lambda b, p, i: (b, p, 0, 0, 0))]
    return pl.pallas_call(
        functools.partial(_attn_kernel, mode=mode, tile=tile, lam_init=lam_init),
        grid=(batch, nblk, nq),
        in_specs=[q_spec, k_spec, v_spec] + extra_specs,
        out_specs=pl.BlockSpec((tile, LANES), lambda b, p, i: (b * nq + i, p)),
        out_shape=jax.ShapeDtypeStruct((batch * seq, nblk * LANES), BF16),
        scratch_shapes=[pltpu.VMEM((2, tile, LANES), BF16),
                        pltpu.VMEM((2, tile, 1), F32),
                        pltpu.VMEM((2, tile, 1), F32),
                        pltpu.VMEM((2, tile, LANES), F32)],
        compiler_params=_params("parallel", "parallel", "parallel"),
        name=mode + "_attention",
    )(qkv, qkv, qkv, *extra)


def _s5_kernel(u_ref, bh_ref, ch_ref, pw_ref, d_ref, gw_ref, gb_ref, o_ref,
               up_ref, bu_ref, xb_ref, yp_ref, carry_ref):
    seg, nseg = SSM_SEG, SUBLANES
    half = bh_ref.shape[1]
    nst = bh_ref.shape[2]
    ncx = nst // 2

    @pl.when(pl.program_id(1) == 0)
    def _():
        carry_ref[...] = jnp.zeros_like(carry_ref)

    nlb = up_ref.shape[0]
    u = u_ref[...].astype(F32)
    for lb in range(nlb):
        for s in range(nseg):
            up_ref[lb, pl.ds(s, seg, stride=nseg), :] = (
                u[s * seg:(s + 1) * seg, lb * LANES:(lb + 1) * LANES])
    ub = jnp.concatenate([up_ref[lb] for lb in range(nlb)], axis=1).astype(BF16)
    for hf in range(2):
        bu_ref[:, hf * nst:(hf + 1) * nst] = jnp.dot(
            ub[:, hf * half:(hf + 1) * half], bh_ref[hf], preferred_element_type=F32)

    sub = lax.broadcasted_iota(jnp.int32, (nseg, SSM_COLS), 0)
    for hf in range(2):
        for c0 in range(0, ncx, SSM_COLS):
            re = slice(hf * nst + c0, hf * nst + c0 + SSM_COLS)
            im = slice(hf * nst + ncx + c0, hf * nst + ncx + c0 + SSM_COLS)
            ar = jnp.broadcast_to(pw_ref[0:1, re], (nseg, SSM_COLS))
            ai = jnp.broadcast_to(pw_ref[0:1, im], (nseg, SSM_COLS))

            def scan_step(t, x, re=re, im=im, ar=ar, ai=ai):
                xr, xi = x
                rows = pl.ds(pl.multiple_of(t * nseg, nseg), nseg)
                nr = ar * xr - ai * xi + bu_ref[rows, re]
                ni = ar * xi + ai * xr + bu_ref[rows, im]
                bu_ref[rows, re] = nr
                bu_ref[rows, im] = ni
                return nr, ni

            zeros = jnp.zeros((nseg, SSM_COLS), F32)
            er, ei = lax.fori_loop(0, seg, scan_step, (zeros, zeros))

            a_seg_r = pw_ref[seg - 1:seg, re]
            a_seg_i = pw_ref[seg - 1:seg, im]
            cr = carry_ref[0:1, re]
            ci = carry_ref[0:1, im]
            cmr = jnp.zeros((nseg, SSM_COLS), F32)
            cmi = jnp.zeros((nseg, SSM_COLS), F32)
            for s in range(nseg):
                cmr = jnp.where(sub == s, jnp.broadcast_to(cr, cmr.shape), cmr)
                cmi = jnp.where(sub == s, jnp.broadcast_to(ci, cmi.shape), cmi)
                nr = a_seg_r * cr - a_seg_i * ci + er[s:s + 1, :]
                ni = a_seg_r * ci + a_seg_i * cr + ei[s:s + 1, :]
                cr, ci = nr, ni
            carry_ref[:, re] = jnp.broadcast_to(cr, (nseg, SSM_COLS))
            carry_ref[:, im] = jnp.broadcast_to(ci, (nseg, SSM_COLS))

            def fix_step(t, carry, re=re, im=im, cmr=cmr, cmi=cmi):
                rows = pl.ds(pl.multiple_of(t * nseg, nseg), nseg)
                pr = pw_ref[pl.ds(t, 1), re]
                pi = pw_ref[pl.ds(t, 1), im]
                xr = bu_ref[rows, re] + (pr * cmr - pi * cmi)
                xi = bu_ref[rows, im] + (pr * cmi + pi * cmr)
                bu_ref[rows, re] = xr
                bu_ref[rows, im] = xi
                return carry

            lax.fori_loop(0, seg, fix_step, 0)

    xb_ref[...] = bu_ref[...].astype(BF16)
    for hf in range(2):
        yh = jnp.dot(xb_ref[:, hf * nst:(hf + 1) * nst], ch_ref[hf],
                     preferred_element_type=F32)
        for lb in range(half // LANES):
            yp_ref[hf * (half // LANES) + lb] = yh[:, lb * LANES:(lb + 1) * LANES]
    y = jnp.concatenate(
        [jnp.concatenate([yp_ref[lb, pl.ds(s, seg, stride=nseg), :] for s in range(nseg)], axis=0)
         for lb in range(nlb)], axis=1)
    y = y + d_ref[...] * u_ref[...].astype(F32)
    y = jax.nn.gelu(y)
    gate = jnp.dot(y.astype(BF16), gw_ref[...], preferred_element_type=F32) + gb_ref[...]
    o_ref[...] = (y * jax.nn.sigmoid(gate)).astype(o_ref.dtype)


def _s5(proj, u_block, bh, ch, pw, d_skip, glu_w, glu_b, batch, seq):
    width = bh.shape[0] * bh.shape[1]
    nstate = bh.shape[0] * bh.shape[2]
    rows = SSM_ROWS
    nc = seq // rows
    const2 = lambda b, i: (0, 0)
    const3 = lambda b, i: (0, 0, 0)
    return pl.pallas_call(
        _s5_kernel,
        grid=(batch, nc),
        in_specs=[pl.BlockSpec((rows, width), lambda b, i: (b * nc + i, u_block)),
                  pl.BlockSpec(bh.shape, const3),
                  pl.BlockSpec(ch.shape, const3),
                  pl.BlockSpec(pw.shape, const2),
                  pl.BlockSpec((1, width), const2),
                  pl.BlockSpec(glu_w.shape, const2),
                  pl.BlockSpec((1, width), const2)],
        out_specs=pl.BlockSpec((rows, width), lambda b, i: (b * nc + i, 0)),
        out_shape=jax.ShapeDtypeStruct((batch * seq, width), BF16),
        scratch_shapes=[pltpu.VMEM((width // LANES, rows, LANES), F32),
                        pltpu.VMEM((rows, nstate), F32),
                        pltpu.VMEM((rows, nstate), BF16),
                        pltpu.VMEM((width // LANES, rows, LANES), F32),
                        pltpu.VMEM((SUBLANES, nstate), F32)],
        compiler_params=_params("parallel", "arbitrary"),
        name="s5_mixer",
    )(proj, bh, ch, pw, d_skip, glu_w, glu_b)


def _s5_tables(a_re, a_im, log_dt, b_re, b_im, c_re, c_im):
    groups, nstate = a_re.shape
    gh = groups // 2
    dt = jnp.exp(log_dt.astype(F32))[:, None]
    a_re = a_re.astype(F32)
    a_im = a_im.astype(F32)
    mag = jnp.exp(a_re * dt)
    ab_re = mag * jnp.cos(a_im * dt)
    ab_im = mag * jnp.sin(a_im * dt)
    den = jnp.square(a_re) + jnp.square(a_im)
    nr = ab_re - 1.0
    ni = ab_im
    g_re = (nr * a_re + ni * a_im) / den
    g_im = (ni * a_re - nr * a_im) / den
    b_re = b_re.astype(F32)
    b_im = b_im.astype(F32)
    bb_re = g_re[..., None] * b_re - g_im[..., None] * b_im
    bb_im = g_re[..., None] * b_im + g_im[..., None] * b_re
    eye = jnp.eye(gh, dtype=F32)

    def in_map(bb):
        return jnp.einsum('gph,gk->ghkp', bb, eye).reshape(gh * SSM_GROUP, gh * nstate)

    def out_map(c):
        return jnp.einsum('ghp,gk->gpkh', c, eye).reshape(gh * nstate, gh * SSM_GROUP)

    bh = jnp.stack([jnp.concatenate([in_map(bb_re[h * gh:(h + 1) * gh]),
                                     in_map(bb_im[h * gh:(h + 1) * gh])], axis=1)
                    for h in range(2)]).astype(BF16)
    ch = jnp.stack([jnp.concatenate([out_map(c_re[h * gh:(h + 1) * gh].astype(F32)),
                                     -out_map(c_im[h * gh:(h + 1) * gh].astype(F32))], axis=0)
                    for h in range(2)]).astype(BF16)
    def pw_step(c, _):
        pr, pi = c
        nxt = (pr * ab_re - pi * ab_im, pr * ab_im + pi * ab_re)
        return nxt, c
    _, (pw_re, pw_im) = lax.scan(pw_step, (ab_re, ab_im), None, length=SSM_SEG)
    pw = jnp.concatenate(
        [jnp.concatenate([pw_re[:, h * gh:(h + 1) * gh].reshape(SSM_SEG, gh * nstate),
                          pw_im[:, h * gh:(h + 1) * gh].reshape(SSM_SEG, gh * nstate)], axis=1)
         for h in range(2)], axis=1)
    return bh, ch, pw


def _outproj_ln_kernel(*refs, n_in, alpha):
    a_refs = refs[:n_in]
    w_refs = refs[n_in:2 * n_in]
    x_ref, g_ref, b_ref, o_ref = refs[2 * n_in:]
    acc = alpha * x_ref[...]
    for a_ref, w_ref in zip(a_refs, w_refs):
        acc = acc + jnp.dot(a_ref[...], w_ref[...], preferred_element_type=F32)
    o_ref[...] = _layer_norm(acc, g_ref[...], b_ref[...])


def _outproj_ln(acts, weights, x2, g, b, alpha):
    t, d = x2.shape
    tm = ROW_TILE
    n_in = len(acts)
    in_specs = ([pl.BlockSpec((tm, a.shape[1]), lambda i: (i, 0)) for a in acts]
                + [pl.BlockSpec(w.shape, lambda i: (0, 0)) for w in weights]
                + [pl.BlockSpec((tm, d), lambda i: (i, 0)),
                   pl.BlockSpec((1, d), lambda i: (0, 0)),
                   pl.BlockSpec((1, d), lambda i: (0, 0))])
    return pl.pallas_call(
        functools.partial(_outproj_ln_kernel, n_in=n_in, alpha=alpha),
        grid=(t // tm,),
        in_specs=in_specs,
        out_specs=pl.BlockSpec((tm, d), lambda i: (i, 0)),
        out_shape=jax.ShapeDtypeStruct((t, d), F32),
        compiler_params=_params("parallel"),
        name="outproj_ln",
    )(*acts, *weights, x2, g, b)


def _ffn_kernel(h_ref, w1_ref, w2_ref, g_ref, b_ref, o_ref, hb_ref, acc_ref, *, alpha):
    j = pl.program_id(1)

    @pl.when(j == 0)
    def _():
        hb_ref[...] = h_ref[...].astype(BF16)
        acc_ref[...] = jnp.zeros_like(acc_ref)

    a = jnp.dot(hb_ref[...], w1_ref[...], preferred_element_type=F32)
    a = jnp.square(jnp.maximum(a, 0.0)).astype(BF16)
    acc_ref[...] += jnp.dot(a, w2_ref[...], preferred_element_type=F32)

    @pl.when(j == pl.num_programs(1) - 1)
    def _():
        o_ref[...] = _layer_norm(alpha * h_ref[...] + acc_ref[...], g_ref[...], b_ref[...])


def _ffn(h2, w1, w2, g, b, alpha):
    t, d = h2.shape
    dff = w1.shape[1]
    tm, tf = FFN_ROW_TILE, FFN_COL_TILE
    return pl.pallas_call(
        functools.partial(_ffn_kernel, alpha=alpha),
        grid=(t // tm, dff // tf),
        in_specs=[pl.BlockSpec((tm, d), lambda i, j: (i, 0)),
                  pl.BlockSpec((d, tf), lambda i, j: (0, j)),
                  pl.BlockSpec((tf, d), lambda i, j: (j, 0)),
                  pl.BlockSpec((1, d), lambda i, j: (0, 0)),
                  pl.BlockSpec((1, d), lambda i, j: (0, 0))],
        out_specs=pl.BlockSpec((tm, d), lambda i, j: (i, 0)),
        out_shape=jax.ShapeDtypeStruct((t, d), F32),
        scratch_shapes=[pltpu.VMEM((tm, d), BF16), pltpu.VMEM((tm, d), F32)],
        compiler_params=_params("parallel", "arbitrary"),
        name="ffn_ln",
    )(h2, w1, w2, g, b)


def _rope_tables(seq):
    inv = ROPE_THETA ** (-jnp.arange(0, QK_DIM, 2, dtype=F32) / QK_DIM)
    ang = jnp.arange(seq, dtype=F32)[:, None] * inv[None, :]
    cos, sin = jnp.cos(ang), jnp.sin(ang)
    reps = LANES // QK_DIM
    cos128 = jnp.tile(cos, (1, 2 * reps))
    sin128 = jnp.tile(jnp.concatenate([-sin, sin], axis=1), (1, reps))
    return cos128, sin128


def kernel(x, ab_w_in, ab_lambda, ab_subln_g, ssm_a_re, ssm_a_im, ssm_log_dt, ssm_b_re, ssm_b_im, ssm_c_re, ssm_c_im, ssm_d, ssm_glu_w, ssm_glu_b, ab_w_out, fox_w_in, fox_b_f, fox_w_out, ln1_g, ln1_b, ffn_w1, ffn_w2, ln2_g, ln2_b):
    batch, seq, d_model = x.shape
    depth = ln1_g.shape[0]
    alpha = (2 * depth) ** 0.25
    assert seq % SSM_ROWS == 0 and seq % ROW_TILE == 0 and (batch * seq) % FFN_ROW_TILE == 0

    diff_heads, diff_v = ab_subln_g.shape[1], ab_subln_g.shape[2]
    assert diff_v == LANES == 2 * QK_DIM
    diff_width = diff_heads * diff_v
    ssm_width = ssm_d.shape[1]
    fox_heads = fox_b_f.shape[1]
    assert d_model == fox_heads * QK_DIM and ssm_width == 2 * 16 * SSM_GROUP

    cos128, sin128 = _rope_tables(seq)
    h = x.reshape(batch * seq, d_model)
    for l in range(depth):
        i = l // 2
        if l % 2 == 0:
            lam_init = 0.8 - 0.6 * math.exp(-0.3 * l)
            proj = _proj_even(h, ab_w_in[i].astype(BF16), cos128, sin128, seq, diff_width)
            attn = _attention("diff", proj, batch, seq, diff_heads,
                              (ab_lambda[i].astype(F32),
                               ab_subln_g[i].astype(F32).reshape(1, diff_width)),
                              lam_init=lam_init)
            bh, ch, pw = _s5_tables(ssm_a_re[i], ssm_a_im[i], ssm_log_dt[i], ssm_b_re[i],
                                    ssm_b_im[i], ssm_c_re[i], ssm_c_im[i])
            y_ssm = _s5(proj, 3 * diff_width // ssm_width, bh, ch, pw,
                        ssm_d[i].astype(F32).reshape(1, ssm_width),
                        ssm_glu_w[i].astype(BF16),
                        ssm_glu_b[i].astype(F32).reshape(1, ssm_width), batch, seq)
            w_out = ab_w_out[i].astype(BF16)
            acts = [attn, y_ssm]
            weights = [w_out[:diff_width], w_out[diff_width:]]
        else:
            w_in = fox_w_in[i]
            w_pad = jnp.pad(w_in, ((0, 0), (0, LANES - fox_heads))).astype(BF16)
            qkv, flog = _proj_odd(h, w_pad, d_model)
            b_pad = jnp.pad(fox_b_f[i].astype(F32), (0, LANES - fox_heads)).reshape(1, LANES)
            fcum = _fgate(flog, b_pad, batch, seq)
            fcum_t = (fcum[:, :fox_heads]
                      .reshape(batch, seq // ATT_TILE, ATT_TILE, fox_heads // 2, 2)
                      .transpose(0, 3, 1, 4, 2))
            attn = _attention("fox", qkv, batch, seq, fox_heads // 2, (fcum, fcum_t))
            acts = [attn]
            weights = [fox_w_out[i].astype(BF16)]
        h = _outproj_ln(acts, weights, h, ln1_g[l].reshape(1, d_model),
                        ln1_b[l].reshape(1, d_model), alpha)
        h = _ffn(h, ffn_w1[l].astype(BF16), ffn_w2[l].astype(BF16),
                 ln2_g[l].reshape(1, d_model), ln2_b[l].reshape(1, d_model), alpha)
    return h.reshape(batch, seq, d_model)
```

```python
import functools
import math

import jax
import jax.numpy as jnp
from jax import lax
from jax.experimental import pallas as pl
from jax.experimental.pallas import tpu as pltpu

F32 = jnp.float32
BF16 = jnp.bfloat16

LANES = 128
SUBLANES = 8
VMEM_LIMIT = 48 * 1024 * 1024

CHUNK = 64
QK_DIM = 64
ROPE_THETA = 10000.0
LN_EPS = 1e-5
RMS_EPS = 1e-6
LOG2E = math.log2(math.e)
QK_SCALE_LOG2 = QK_DIM ** -0.5 * LOG2E
SSM_GROUP = 16
SSM_STATE = 64
SSM_SEG = 64
SSM_ROWS = SUBLANES * SSM_SEG
SSM_COLS = 512
ATT_TILE = 512
ROW_TILE = 512
FFN_ROW_TILE = 1024
FFN_COL_TILE = 512
GATE_TILE = 256


def _params(*sem):
    return pltpu.CompilerParams(dimension_semantics=sem, vmem_limit_bytes=VMEM_LIMIT)


def _layer_norm(y, g, b):
    mu = jnp.mean(y, axis=-1, keepdims=True)
    d = y - mu
    var = jnp.mean(d * d, axis=-1, keepdims=True)
    return d * lax.rsqrt(var + LN_EPS) * g + b


def _proj_even_kernel(x_ref, w_ref, cos_ref, sin_ref, o_ref, *, qk_width, scale):
    xb = x_ref[...].astype(BF16)
    cosv = cos_ref[...]
    sinv = sin_ref[...]
    lane = lax.broadcasted_iota(jnp.int32, cosv.shape, 1)
    first_half = (lane % QK_DIM) < (QK_DIM // 2)
    n_out = o_ref.shape[1]
    for c0 in range(0, n_out, 512):
        acc = jnp.dot(xb, w_ref[:, c0:c0 + 512], preferred_element_type=F32)
        for l0 in range(0, 512, LANES):
            a = acc[:, l0:l0 + LANES]
            col = c0 + l0
            if col < 2 * qk_width:
                partner = jnp.where(first_half,
                                    pltpu.roll(a, LANES - QK_DIM // 2, 1),
                                    pltpu.roll(a, QK_DIM // 2, 1))
                a = a * cosv + partner * sinv
                if col < qk_width:
                    a = a * scale
            o_ref[:, col:col + LANES] = a.astype(o_ref.dtype)


def _proj_even(x2, w, cos128, sin128, seq, qk_width):
    t, d = x2.shape
    n = w.shape[1]
    tm = ROW_TILE
    nseq = seq // tm
    return pl.pallas_call(
        functools.partial(_proj_even_kernel, qk_width=qk_width, scale=QK_SCALE_LOG2),
        grid=(t // tm,),
        in_specs=[pl.BlockSpec((tm, d), lambda i: (i, 0)),
                  pl.BlockSpec((d, n), lambda i: (0, 0)),
                  pl.BlockSpec((tm, LANES), lambda i: (i % nseq, 0)),
                  pl.BlockSpec((tm, LANES), lambda i: (i % nseq, 0))],
        out_specs=pl.BlockSpec((tm, n), lambda i: (i, 0)),
        out_shape=jax.ShapeDtypeStruct((t, n), BF16),
        compiler_params=_params("parallel"),
        name="proj_even",
    )(x2, w, cos128, sin128)


def _proj_odd_kernel(x_ref, w_ref, o_ref, f_ref, *, q_width, scale):
    xb = x_ref[...].astype(BF16)
    n_out = o_ref.shape[1]
    for c0 in range(0, n_out, 512):
        acc = jnp.dot(xb, w_ref[:, c0:c0 + 512], preferred_element_type=F32)
        if c0 < q_width:
            acc = acc * scale
        o_ref[:, c0:c0 + 512] = acc.astype(o_ref.dtype)
    f_ref[...] = jnp.dot(xb, w_ref[:, n_out:n_out + LANES], preferred_element_type=F32)


def _proj_odd(x2, w, q_width):
    t, d = x2.shape
    n = w.shape[1] - LANES
    tm = ROW_TILE
    return pl.pallas_call(
        functools.partial(_proj_odd_kernel, q_width=q_width, scale=QK_SCALE_LOG2),
        grid=(t // tm,),
        in_specs=[pl.BlockSpec((tm, d), lambda i: (i, 0)),
                  pl.BlockSpec((d, n + LANES), lambda i: (0, 0))],
        out_specs=[pl.BlockSpec((tm, n), lambda i: (i, 0)),
                   pl.BlockSpec((tm, LANES), lambda i: (i, 0))],
        out_shape=[jax.ShapeDtypeStruct((t, n), BF16),
                   jax.ShapeDtypeStruct((t, LANES), F32)],
        compiler_params=_params("parallel"),
        name="proj_odd",
    )(x2, w)


def _fgate_kernel(f_ref, b_ref, o_ref, carry_ref):
    @pl.when(pl.program_id(1) == 0)
    def _():
        carry_ref[...] = jnp.zeros_like(carry_ref)

    logf = jax.nn.log_sigmoid(f_ref[...] + b_ref[...]) * LOG2E
    n = logf.shape[0]
    row = lax.broadcasted_iota(jnp.int32, (n, n), 0)
    col = lax.broadcasted_iota(jnp.int32, (n, n), 1)
    tri = (col <= row).astype(F32)
    cum = jnp.dot(tri, logf, preferred_element_type=F32,
                  precision=lax.Precision.HIGHEST) + carry_ref[0:1, :]
    o_ref[...] = cum
    carry_ref[...] = jnp.broadcast_to(cum[n - 1:n, :], carry_ref.shape)


def _fgate(flog, b_pad, batch, seq):
    tg = GATE_TILE
    nb = seq // tg
    return pl.pallas_call(
        _fgate_kernel,
        grid=(batch, nb),
        in_specs=[pl.BlockSpec((tg, LANES), lambda b, i: (b * nb + i, 0)),
                  pl.BlockSpec((1, LANES), lambda b, i: (0, 0))],
        out_specs=pl.BlockSpec((tg, LANES), lambda b, i: (b * nb + i, 0)),
        out_shape=jax.ShapeDtypeStruct(flog.shape, F32),
        scratch_shapes=[pltpu.VMEM((SUBLANES, LANES), F32)],
        compiler_params=_params("parallel", "arbitrary"),
        name="fgate_cumsum",
    )(flog, b_pad)


def _attn_kernel(*refs, mode, tile, lam_init):
    if mode == "diff":
        q_ref, k_ref, v_ref, lam_ref, g_ref, o_ref, qs_ref, m_ref, l_ref, acc_ref = refs
    else:
        q_ref, k_ref, v_ref, fq_ref, fk_ref, o_ref, qs_ref, m_ref, l_ref, acc_ref, fqs_ref = refs
    qi = pl.program_id(2)
    q = q_ref[...]
    lane = lax.broadcasted_iota(jnp.int32, q.shape, 1)
    low = lane < QK_DIM
    zero = jnp.zeros_like(q)
    qs_ref[0:tile, :] = jnp.where(low, q, zero)
    qs_ref[tile:2 * tile, :] = jnp.where(low, zero, q)
    m_ref[...] = jnp.full(m_ref.shape, -jnp.inf, F32)
    l_ref[...] = jnp.zeros_like(l_ref)
    acc_ref[...] = jnp.zeros_like(acc_ref)

    if mode == "fox":
        pair = pl.program_id(1)
        fblk = fq_ref[...]
        for mp in range(2):
            fq = jnp.sum(jnp.where(lane == 2 * pair + mp, fblk, 0.0), axis=1, keepdims=True)
            fqs_ref[mp * tile:(mp + 1) * tile, :] = jnp.broadcast_to(fq, (tile, LANES))

    row = lax.broadcasted_iota(jnp.int32, (2 * tile, tile), 0) % tile
    col = lax.broadcasted_iota(jnp.int32, (2 * tile, tile), 1)
    if mode == "diff":
        diag_mask = (col // CHUNK) <= (row // CHUNK)
    else:
        diag_mask = col <= row
    nlb = tile // LANES

    def step(ki, masked):
        k0 = pl.multiple_of(ki * tile, tile)
        k = k_ref[pl.ds(k0, tile), :]
        v = v_ref[pl.ds(k0, tile), :]
        s = lax.dot_general(qs_ref[...], k, (((1,), (1,)), ((), ())),
                            preferred_element_type=F32)
        if mode == "fox":
            s = jnp.concatenate([s[0:tile] - fk_ref[ki, 0:1, :],
                                 s[tile:2 * tile] - fk_ref[ki, 1:2, :]], axis=0)
        if masked:
            s = jnp.where(diag_mask, s, -jnp.inf)
        m_old = m_ref[...]
        row_max = jnp.broadcast_to(jnp.max(s, axis=1, keepdims=True), m_old.shape)
        if mode == "fox":
            fq = fqs_ref[...]
            m_new = jnp.maximum(m_old, row_max + fq)
            shift = fq - m_new
        else:
            m_new = jnp.maximum(m_old, row_max)
            shift = -m_new
        p = jnp.exp2(s + jnp.concatenate([shift] * nlb, axis=1))
        alpha = jnp.exp2(m_old - m_new)
        psum = p[:, 0:LANES]
        for lb in range(1, nlb):
            psum = psum + p[:, lb * LANES:(lb + 1) * LANES]
        l_ref[...] = alpha * l_ref[...] + psum
        acc_ref[...] = alpha * acc_ref[...] + jnp.dot(p.astype(BF16), v,
                                                      preferred_element_type=F32)
        m_ref[...] = m_new

    def body(ki, carry):
        step(ki, False)
        return carry

    lax.fori_loop(0, qi, body, 0)
    step(qi, True)

    o = acc_ref[...] / jnp.sum(l_ref[...], axis=1, keepdims=True)
    o0 = o[0:tile]
    o1 = o[tile:2 * tile]
    if mode == "diff":
        lv = lam_ref[...]
        lam = (jnp.exp(jnp.sum(lv[0:1] * lv[1:2], axis=1, keepdims=True))
               - jnp.exp(jnp.sum(lv[2:3] * lv[3:4], axis=1, keepdims=True)) + lam_init)
        d = o0 - lam * o1
        d = d * lax.rsqrt(jnp.mean(d * d, axis=-1, keepdims=True) + RMS_EPS)
        o_ref[...] = (d * g_ref[...] * (1.0 - lam_init)).astype(o_ref.dtype)
    else:
        o_ref[...] = jnp.where(low, o0, o1).astype(o_ref.dtype)


def _attention(mode, qkv, batch, seq, nblk, extra, lam_init=0.0):
    tile = ATT_TILE
    nq = seq // tile
    q_spec = pl.BlockSpec((tile, LANES), lambda b, p, i: (b * nq + i, p))
    k_spec = pl.BlockSpec((seq, LANES), lambda b, p, i: (b, nblk + p))
    v_spec = pl.BlockSpec((seq, LANES), lambda b, p, i: (b, 2 * nblk + p))
    scratch = [pltpu.VMEM((2 * tile, LANES), BF16),
               pltpu.VMEM((2 * tile, LANES), F32),
               pltpu.VMEM((2 * tile, LANES), F32),
               pltpu.VMEM((2 * tile, LANES), F32)]
    if mode == "diff":
        lam_vecs, gains = extra
        extra_specs = [pl.BlockSpec(lam_vecs.shape, lambda b, p, i: (0, 0)),
                       pl.BlockSpec((1, LANES), lambda b, p, i: (0, p))]
    else:
        fcum, fcum_t = extra
        extra_specs = [pl.BlockSpec((tile, LANES), lambda b, p, i: (b * nq + i, 0)),
                       pl.BlockSpec((None, None, nq, 2, tile),
                                    lambda b, p, i: (b, p, 0, 0, 0))]
        scratch.append(pltpu.VMEM((2 * tile, LANES), F32))
    return pl.pallas_call(
        functools.partial(_attn_kernel, mode=mode, tile=tile, lam_init=lam_init),
        grid=(batch, nblk, nq),
        in_specs=[q_spec, k_spec, v_spec] + extra_specs,
        out_specs=pl.BlockSpec((tile, LANES), lambda b, p, i: (b * nq + i, p)),
        out_shape=jax.ShapeDtypeStruct((batch * seq, nblk * LANES), BF16),
        scratch_shapes=scratch,
        compiler_params=_params("parallel", "parallel", "parallel"),
        name=mode + "_attention",
    )(qkv, qkv, qkv, *extra)


def _s5_kernel(u_ref, bh_ref, ch_ref, pw_ref, d_ref, gw_ref, gb_ref, o_ref,
               up_ref, bu_ref, xb_ref, yp_ref, carry_ref):
    seg, nseg = SSM_SEG, SUBLANES
    half = bh_ref.shape[1]
    nst = bh_ref.shape[2]
    ncx = nst // 2

    @pl.when(pl.program_id(1) == 0)
    def _():
        carry_ref[...] = jnp.zeros_like(carry_ref)

    nlb = up_ref.shape[0]
    u = u_ref[...].astype(F32)
    for lb in range(nlb):
        for s in range(nseg):
            up_ref[lb, pl.ds(s, seg, stride=nseg), :] = (
                u[s * seg:(s + 1) * seg, lb * LANES:(lb + 1) * LANES])
    ub = jnp.concatenate([up_ref[lb] for lb in range(nlb)], axis=1).astype(BF16)
    for hf in range(2):
        bu_ref[:, hf * nst:(hf + 1) * nst] = jnp.dot(
            ub[:, hf * half:(hf + 1) * half], bh_ref[hf], preferred_element_type=F32)

    sub = lax.broadcasted_iota(jnp.int32, (nseg, SSM_COLS), 0)
    for hf in range(2):
        for c0 in range(0, ncx, SSM_COLS):
            re = slice(hf * nst + c0, hf * nst + c0 + SSM_COLS)
            im = slice(hf * nst + ncx + c0, hf * nst + ncx + c0 + SSM_COLS)
            ar = jnp.broadcast_to(pw_ref[0:1, re], (nseg, SSM_COLS))
            ai = jnp.broadcast_to(pw_ref[0:1, im], (nseg, SSM_COLS))

            def scan_step(t, x, re=re, im=im, ar=ar, ai=ai):
                xr, xi = x
                rows = pl.ds(pl.multiple_of(t * nseg, nseg), nseg)
                nr = ar * xr - ai * xi + bu_ref[rows, re]
                ni = ar * xi + ai * xr + bu_ref[rows, im]
                bu_ref[rows, re] = nr
                bu_ref[rows, im] = ni
                return nr, ni

            zeros = jnp.zeros((nseg, SSM_COLS), F32)
            er, ei = lax.fori_loop(0, seg, scan_step, (zeros, zeros))

            a_seg_r = pw_ref[seg - 1:seg, re]
            a_seg_i = pw_ref[seg - 1:seg, im]
            cr = carry_ref[0:1, re]
            ci = carry_ref[0:1, im]
            cmr = jnp.zeros((nseg, SSM_COLS), F32)
            cmi = jnp.zeros((nseg, SSM_COLS), F32)
            for s in range(nseg):
                cmr = jnp.where(sub == s, jnp.broadcast_to(cr, cmr.shape), cmr)
                cmi = jnp.where(sub == s, jnp.broadcast_to(ci, cmi.shape), cmi)
                nr = a_seg_r * cr - a_seg_i * ci + er[s:s + 1, :]
                ni = a_seg_r * ci + a_seg_i * cr + ei[s:s + 1, :]
                cr, ci = nr, ni
            carry_ref[:, re] = jnp.broadcast_to(cr, (nseg, SSM_COLS))
            carry_ref[:, im] = jnp.broadcast_to(ci, (nseg, SSM_COLS))

            def fix_step(t, carry, re=re, im=im, cmr=cmr, cmi=cmi):
                rows = pl.ds(pl.multiple_of(t * nseg, nseg), nseg)
                pr = pw_ref[pl.ds(t, 1), re]
                pi = pw_ref[pl.ds(t, 1), im]
                xr = bu_ref[rows, re] + (pr * cmr - pi * cmi)
                xi = bu_ref[rows, im] + (pr * cmi + pi * cmr)
                bu_ref[rows, re] = xr
                bu_ref[rows, im] = xi
                return carry

            lax.fori_loop(0, seg, fix_step, 0)

    xb_ref[...] = bu_ref[...].astype(BF16)
    for hf in range(2):
        yh = jnp.dot(xb_ref[:, hf * nst:(hf + 1) * nst], ch_ref[hf],
                     preferred_element_type=F32)
        for lb in range(half // LANES):
            yp_ref[hf * (half // LANES) + lb] = yh[:, lb * LANES:(lb + 1) * LANES]
    y = jnp.concatenate(
        [jnp.concatenate([yp_ref[lb, pl.ds(s, seg, stride=nseg), :] for s in range(nseg)], axis=0)
         for lb in range(nlb)], axis=1)
    y = y + d_ref[...] * u_ref[...].astype(F32)
    y = jax.nn.gelu(y)
    gate = jnp.dot(y.astype(BF16), gw_ref[...], preferred_element_type=F32) + gb_ref[...]
    o_ref[...] = (y * jax.nn.sigmoid(gate)).astype(o_ref.dtype)


def _s5(proj, u_block, bh, ch, pw, d_skip, glu_w, glu_b, batch, seq):
    width = bh.shape[0] * bh.shape[1]
    nstate = bh.shape[0] * bh.shape[2]
    rows = SSM_ROWS
    nc = seq // rows
    const2 = lambda b, i: (0, 0)
    const3 = lambda b, i: (0, 0, 0)
    return pl.pallas_call(
        _s5_kernel,
        grid=(batch, nc),
        in_specs=[pl.BlockSpec((rows, width), lambda b, i: (b * nc + i, u_block)),
                  pl.BlockSpec(bh.shape, const3),
                  pl.BlockSpec(ch.shape, const3),
                  pl.BlockSpec(pw.shape, const2),
                  pl.BlockSpec((1, width), const2),
                  pl.BlockSpec(glu_w.shape, const2),
                  pl.BlockSpec((1, width), const2)],
        out_specs=pl.BlockSpec((rows, width), lambda b, i: (b * nc + i, 0)),
        out_shape=jax.ShapeDtypeStruct((batch * seq, width), BF16),
        scratch_shapes=[pltpu.VMEM((width // LANES, rows, LANES), F32),
                        pltpu.VMEM((rows, nstate), F32),
                        pltpu.VMEM((rows, nstate), BF16),
                        pltpu.VMEM((width // LANES, rows, LANES), F32),
                        pltpu.VMEM((SUBLANES, nstate), F32)],
        compiler_params=_params("parallel", "arbitrary"),
        name="s5_mixer",
    )(proj, bh, ch, pw, d_skip, glu_w, glu_b)


def _s5_tables(a_re, a_im, log_dt, b_re, b_im, c_re, c_im):
    groups, nstate = a_re.shape
    gh = groups // 2
    dt = jnp.exp(log_dt.astype(F32))[:, None]
    a_re = a_re.astype(F32)
    a_im = a_im.astype(F32)
    mag = jnp.exp(a_re * dt)
    ab_re = mag * jnp.cos(a_im * dt)
    ab_im = mag * jnp.sin(a_im * dt)
    den = jnp.square(a_re) + jnp.square(a_im)
    nr = ab_re - 1.0
    ni = ab_im
    g_re = (nr * a_re + ni * a_im) / den
    g_im = (ni * a_re - nr * a_im) / den
    b_re = b_re.astype(F32)
    b_im = b_im.astype(F32)
    bb_re = g_re[..., None] * b_re - g_im[..., None] * b_im
    bb_im = g_re[..., None] * b_im + g_im[..., None] * b_re
    eye = jnp.eye(gh, dtype=F32)

    def in_map(bb):
        return jnp.einsum('gph,gk->ghkp', bb, eye).reshape(gh * SSM_GROUP, gh * nstate)

    def out_map(c):
        return jnp.einsum('ghp,gk->gpkh', c, eye).reshape(gh * nstate, gh * SSM_GROUP)

    bh = jnp.stack([jnp.concatenate([in_map(bb_re[h * gh:(h + 1) * gh]),
                                     in_map(bb_im[h * gh:(h + 1) * gh])], axis=1)
                    for h in range(2)]).astype(BF16)
    ch = jnp.stack([jnp.concatenate([out_map(c_re[h * gh:(h + 1) * gh].astype(F32)),
                                     -out_map(c_im[h * gh:(h + 1) * gh].astype(F32))], axis=0)
                    for h in range(2)]).astype(BF16)
    def pw_step(c, _):
        pr, pi = c
        nxt = (pr * ab_re - pi * ab_im, pr * ab_im + pi * ab_re)
        return nxt, c
    _, (pw_re, pw_im) = lax.scan(pw_step, (ab_re, ab_im), None, length=SSM_SEG)
    pw = jnp.concatenate(
        [jnp.concatenate([pw_re[:, h * gh:(h + 1) * gh].reshape(SSM_SEG, gh * nstate),
                          pw_im[:, h * gh:(h + 1) * gh].reshape(SSM_SEG, gh * nstate)], axis=1)
         for h in range(2)], axis=1)
    return bh, ch, pw


def _outproj_ln_kernel(*refs, n_in, alpha):
    a_refs = refs[:n_in]
    w_refs = refs[n_in:2 * n_in]
    x_ref, g_ref, b_ref, o_ref = refs[2 * n_in:]
    acc = alpha * x_ref[...]
    for a_ref, w_ref in zip(a_refs, w_refs):
        acc = acc + jnp.dot(a_ref[...], w_ref[...], preferred_element_type=F32)
    o_ref[...] = _layer_norm(acc, g_ref[...], b_ref[...])


def _outproj_ln(acts, weights, x2, g, b, alpha):
    t, d = x2.shape
    tm = ROW_TILE
    n_in = len(acts)
    in_specs = ([pl.BlockSpec((tm, a.shape[1]), lambda i: (i, 0)) for a in acts]
                + [pl.BlockSpec(w.shape, lambda i: (0, 0)) for w in weights]
                + [pl.BlockSpec((tm, d), lambda i: (i, 0)),
                   pl.BlockSpec((1, d), lambda i: (0, 0)),
                   pl.BlockSpec((1, d), lambda i: (0, 0))])
    return pl.pallas_call(
        functools.partial(_outproj_ln_kernel, n_in=n_in, alpha=alpha),
        grid=(t // tm,),
        in_specs=in_specs,
        out_specs=pl.BlockSpec((tm, d), lambda i: (i, 0)),
        out_shape=jax.ShapeDtypeStruct((t, d), F32),
        compiler_params=_params("parallel"),
        name="outproj_ln",
    )(*acts, *weights, x2, g, b)


def _ffn_kernel(h_ref, w1_ref, w2_ref, g_ref, b_ref, o_ref, hb_ref, acc_ref, *, alpha):
    j = pl.program_id(1)

    @pl.when(j == 0)
    def _():
        hb_ref[...] = h_ref[...].astype(BF16)
        acc_ref[...] = jnp.zeros_like(acc_ref)

    a = jnp.dot(hb_ref[...], w1_ref[...], preferred_element_type=F32)
    a = jnp.square(jnp.maximum(a, 0.0)).astype(BF16)
    acc_ref[...] += jnp.dot(a, w2_ref[...], preferred_element_type=F32)

    @pl.when(j == pl.num_programs(1) - 1)
    def _():
        o_ref[...] = _layer_norm(alpha * h_ref[...] + acc_ref[...], g_ref[...], b_ref[...])


def _ffn(h2, w1, w2, g, b, alpha):
    t, d = h2.shape
    dff = w1.shape[1]
    tm, tf = FFN_ROW_TILE, FFN_COL_TILE
    return pl.pallas_call(
        functools.partial(_ffn_kernel, alpha=alpha),
        grid=(t // tm, dff // tf),
        in_specs=[pl.BlockSpec((tm, d), lambda i, j: (i, 0)),
                  pl.BlockSpec((d, tf), lambda i, j: (0, j)),
                  pl.BlockSpec((tf, d), lambda i, j: (j, 0)),
                  pl.BlockSpec((1, d), lambda i, j: (0, 0)),
                  pl.BlockSpec((1, d), lambda i, j: (0, 0))],
        out_specs=pl.BlockSpec((tm, d), lambda i, j: (i, 0)),
        out_shape=jax.ShapeDtypeStruct((t, d), F32),
        scratch_shapes=[pltpu.VMEM((tm, d), BF16), pltpu.VMEM((tm, d), F32)],
        compiler_params=_params("parallel", "arbitrary"),
        name="ffn_ln",
    )(h2, w1, w2, g, b)


def _rope_tables(seq):
    inv = ROPE_THETA ** (-jnp.arange(0, QK_DIM, 2, dtype=F32) / QK_DIM)
    ang = jnp.arange(seq, dtype=F32)[:, None] * inv[None, :]
    cos, sin = jnp.cos(ang), jnp.sin(ang)
    reps = LANES // QK_DIM
    cos128 = jnp.tile(cos, (1, 2 * reps))
    sin128 = jnp.tile(jnp.concatenate([-sin, sin], axis=1), (1, reps))
    return cos128, sin128


def kernel(x, ab_w_in, ab_lambda, ab_subln_g, ssm_a_re, ssm_a_im, ssm_log_dt, ssm_b_re, ssm_b_im, ssm_c_re, ssm_c_im, ssm_d, ssm_glu_w, ssm_glu_b, ab_w_out, fox_w_in, fox_b_f, fox_w_out, ln1_g, ln1_b, ffn_w1, ffn_w2, ln2_g, ln2_b):
    batch, seq, d_model = x.shape
    depth = ln1_g.shape[0]
    alpha = (2 * depth) ** 0.25
    assert seq % SSM_ROWS == 0 and seq % ROW_TILE == 0 and (batch * seq) % FFN_ROW_TILE == 0

    diff_heads, diff_v = ab_subln_g.shape[1], ab_subln_g.shape[2]
    assert diff_v == LANES == 2 * QK_DIM
    diff_width = diff_heads * diff_v
    ssm_width = ssm_d.shape[1]
    fox_heads = fox_b_f.shape[1]
    assert d_model == fox_heads * QK_DIM and ssm_width == 2 * 16 * SSM_GROUP

    cos128, sin128 = _rope_tables(seq)
    h = x.reshape(batch * seq, d_model)
    for l in range(depth):
        i = l // 2
        if l % 2 == 0:
            lam_init = 0.8 - 0.6 * math.exp(-0.3 * l)
            proj = _proj_even(h, ab_w_in[i].astype(BF16), cos128, sin128, seq, diff_width)
            attn = _attention("diff", proj, batch, seq, diff_heads,
                              (ab_lambda[i].astype(F32),
                               ab_subln_g[i].astype(F32).reshape(1, diff_width)),
                              lam_init=lam_init)
            bh, ch, pw = _s5_tables(ssm_a_re[i], ssm_a_im[i], ssm_log_dt[i], ssm_b_re[i],
                                    ssm_b_im[i], ssm_c_re[i], ssm_c_im[i])
            y_ssm = _s5(proj, 3 * diff_width // ssm_width, bh, ch, pw,
                        ssm_d[i].astype(F32).reshape(1, ssm_width),
                        ssm_glu_w[i].astype(BF16),
                        ssm_glu_b[i].astype(F32).reshape(1, ssm_width), batch, seq)
            w_out = ab_w_out[i].astype(BF16)
            acts = [attn, y_ssm]
            weights = [w_out[:diff_width], w_out[diff_width:]]
        else:
            w_in = fox_w_in[i]
            w_pad = jnp.pad(w_in, ((0, 0), (0, LANES - fox_heads))).astype(BF16)
            qkv, flog = _proj_odd(h, w_pad, d_model)
            b_pad = jnp.pad(fox_b_f[i].astype(F32), (0, LANES - fox_heads)).reshape(1, LANES)
            fcum = _fgate(flog, b_pad, batch, seq)
            fcum_t = (fcum[:, :fox_heads]
                      .reshape(batch, seq // ATT_TILE, ATT_TILE, fox_heads // 2, 2)
                      .transpose(0, 3, 1, 4, 2))
            attn = _attention("fox", qkv, batch, seq, fox_heads // 2, (fcum, fcum_t))
            acts = [attn]
            weights = [fox_w_out[i].astype(BF16)]
        h = _outproj_ln(acts, weights, h, ln1_g[l].reshape(1, d_model),
                        ln1_b[l].reshape(1, d_model), alpha)
        h = _ffn(h, ffn_w1[l].astype(BF16), ffn_w2[l].astype(BF16),
                 ln2_g[l].reshape(1, d_model), ln2_b[l].reshape(1, d_model), alpha)
    return h.reshape(batch, seq, d_model)
```

```python
import functools
import math

import jax
import jax.numpy as jnp
import numpy as np
from jax import lax
from jax.experimental import pallas as pl
from jax.experimental.pallas import tpu as pltpu

F32 = jnp.float32
BF16 = jnp.bfloat16

LANES = 128
SUBLANES = 8
VMEM_LIMIT = 48 * 1024 * 1024

CHUNK = 64
QK_DIM = 64
ROPE_THETA = 10000.0
LN_EPS = 1e-5
RMS_EPS = 1e-6
LOG2E = math.log2(math.e)
QK_SCALE_LOG2 = QK_DIM ** -0.5 * LOG2E
SSM_GROUP = 16
SSM_STATE = 64
SSM_SEG = 64
SSM_ROWS = SUBLANES * SSM_SEG
SSM_COLS = 512
ATT_TILE = 512
ATT_STREAMS = 2
ATT_ROW_CHUNK = 32
BIAS_PIECES = 3
ROW_TILE = 512
FFN_ROW_TILE = 1024
FFN_COL_TILE = 512
GATE_TILE = 256


def _params(*sem):
    return pltpu.CompilerParams(dimension_semantics=sem, vmem_limit_bytes=VMEM_LIMIT)


def _layer_norm(y, g, b):
    mu = jnp.mean(y, axis=-1, keepdims=True)
    d = y - mu
    var = jnp.mean(d * d, axis=-1, keepdims=True)
    return d * lax.rsqrt(var + LN_EPS) * g + b


def _proj_even_kernel(x_ref, w_ref, cos_ref, sin_ref, o_ref, *, qk_width, scale):
    xb = x_ref[...].astype(BF16)
    cosv = cos_ref[...]
    sinv = sin_ref[...]
    lane = lax.broadcasted_iota(jnp.int32, cosv.shape, 1)
    first_half = (lane % QK_DIM) < (QK_DIM // 2)
    n_out = o_ref.shape[1]
    for c0 in range(0, n_out, 512):
        acc = jnp.dot(xb, w_ref[:, c0:c0 + 512], preferred_element_type=F32)
        for l0 in range(0, 512, LANES):
            a = acc[:, l0:l0 + LANES]
            col = c0 + l0
            if col < 2 * qk_width:
                partner = jnp.where(first_half,
                                    pltpu.roll(a, LANES - QK_DIM // 2, 1),
                                    pltpu.roll(a, QK_DIM // 2, 1))
                a = a * cosv + partner * sinv
                if col < qk_width:
                    a = a * scale
            o_ref[:, col:col + LANES] = a.astype(o_ref.dtype)


def _proj_even(x2, w, cos128, sin128, seq, qk_width):
    t, d = x2.shape
    n = w.shape[1]
    tm = ROW_TILE
    nseq = seq // tm
    return pl.pallas_call(
        functools.partial(_proj_even_kernel, qk_width=qk_width, scale=QK_SCALE_LOG2),
        grid=(t // tm,),
        in_specs=[pl.BlockSpec((tm, d), lambda i: (i, 0)),
                  pl.BlockSpec((d, n), lambda i: (0, 0)),
                  pl.BlockSpec((tm, LANES), lambda i: (i % nseq, 0)),
                  pl.BlockSpec((tm, LANES), lambda i: (i % nseq, 0))],
        out_specs=pl.BlockSpec((tm, n), lambda i: (i, 0)),
        out_shape=jax.ShapeDtypeStruct((t, n), BF16),
        compiler_params=_params("parallel"),
        name="proj_even",
    )(x2, w, cos128, sin128)


def _proj_odd_kernel(x_ref, w_ref, o_ref, f_ref, *, q_width, scale):
    xb = x_ref[...].astype(BF16)
    n_out = o_ref.shape[1]
    for c0 in range(0, n_out, 512):
        acc = jnp.dot(xb, w_ref[:, c0:c0 + 512], preferred_element_type=F32)
        if c0 < q_width:
            acc = acc * scale
        o_ref[:, c0:c0 + 512] = acc.astype(o_ref.dtype)
    f_ref[...] = jnp.dot(xb, w_ref[:, n_out:n_out + LANES], preferred_element_type=F32)


def _proj_odd(x2, w, q_width):
    t, d = x2.shape
    n = w.shape[1] - LANES
    tm = ROW_TILE
    return pl.pallas_call(
        functools.partial(_proj_odd_kernel, q_width=q_width, scale=QK_SCALE_LOG2),
        grid=(t // tm,),
        in_specs=[pl.BlockSpec((tm, d), lambda i: (i, 0)),
                  pl.BlockSpec((d, n + LANES), lambda i: (0, 0))],
        out_specs=[pl.BlockSpec((tm, n), lambda i: (i, 0)),
                   pl.BlockSpec((tm, LANES), lambda i: (i, 0))],
        out_shape=[jax.ShapeDtypeStruct((t, n), BF16),
                   jax.ShapeDtypeStruct((t, LANES), F32)],
        compiler_params=_params("parallel"),
        name="proj_odd",
    )(x2, w)


def _fgate_kernel(f_ref, b_ref, sel_ref, o_ref, aux_ref, carry_ref):
    @pl.when(pl.program_id(1) == 0)
    def _():
        carry_ref[...] = jnp.zeros_like(carry_ref)

    logf = jax.nn.log_sigmoid(f_ref[...] + b_ref[...]) * LOG2E
    n = logf.shape[0]
    row = lax.broadcasted_iota(jnp.int32, (n, n), 0)
    col = lax.broadcasted_iota(jnp.int32, (n, n), 1)
    tri = (col <= row).astype(F32)
    cum = jnp.dot(tri, logf, preferred_element_type=F32,
                  precision=lax.Precision.HIGHEST) + carry_ref[0:1, :]
    o_ref[...] = cum
    carry_ref[...] = jnp.broadcast_to(cum[n - 1:n, :], carry_ref.shape)
    neg = -cum
    hi = neg.astype(BF16)
    rem = neg - hi.astype(F32)
    mid = rem.astype(BF16)
    lo = (rem - mid.astype(F32)).astype(BF16)
    pieces = jnp.concatenate([hi, mid, lo], axis=1)
    aux_ref[...] = jnp.dot(pieces, sel_ref[...], preferred_element_type=F32).astype(aux_ref.dtype)


def _fgate(flog, b_pad, sel, batch, seq):
    tg = GATE_TILE
    nb = seq // tg
    return pl.pallas_call(
        _fgate_kernel,
        grid=(batch, nb),
        in_specs=[pl.BlockSpec((tg, LANES), lambda b, i: (b * nb + i, 0)),
                  pl.BlockSpec((1, LANES), lambda b, i: (0, 0)),
                  pl.BlockSpec(sel.shape, lambda b, i: (0, 0))],
        out_specs=[pl.BlockSpec((tg, LANES), lambda b, i: (b * nb + i, 0)),
                   pl.BlockSpec((tg, sel.shape[1]), lambda b, i: (b * nb + i, 0))],
        out_shape=[jax.ShapeDtypeStruct(flog.shape, F32),
                   jax.ShapeDtypeStruct((flog.shape[0], sel.shape[1]), BF16)],
        scratch_shapes=[pltpu.VMEM((SUBLANES, LANES), F32)],
        compiler_params=_params("parallel", "arbitrary"),
        name="fgate_cumsum",
    )(flog, b_pad, sel)


def _bias_routing(heads, heads_per_block):
    nblock = heads // heads_per_block
    sel = np.zeros((BIAS_PIECES * LANES, nblock * LANES), np.float32)
    for h in range(heads):
        for j in range(BIAS_PIECES):
            sel[j * LANES + h, (h // heads_per_block) * LANES
                + BIAS_PIECES * (h % heads_per_block) + j] = 1.0
    return jnp.asarray(sel, dtype=BF16)


def _attn_kernel(*refs, mode, tile, lam_init):
    if mode == "diff":
        (q_ref, k_ref, v_ref, lam_ref, g_ref, o_ref,
         qs_ref, m_ref, l_ref, acc_ref, shift_ref, alpha_ref, s_ref, p_ref) = refs
    else:
        (q_ref, k_ref, v_ref, fq_ref, kb_ref, o_ref,
         qs_ref, m_ref, l_ref, acc_ref, shift_ref, alpha_ref, s_ref, p_ref, fqs_ref) = refs
    qi = pl.program_id(2)
    nstream = qs_ref.shape[0]
    lane = lax.broadcasted_iota(jnp.int32, (tile, LANES), 1)
    low = lane < QK_DIM
    for st in range(nstream):
        q = q_ref[:, st * LANES:(st + 1) * LANES]
        zero = jnp.zeros_like(q)
        qs_ref[st, 0:tile, 0:LANES] = jnp.where(low, q, zero)
        qs_ref[st, tile:2 * tile, 0:LANES] = jnp.where(low, zero, q)
        if mode == "fox":
            for mp in range(2):
                c0 = BIAS_PIECES * (2 * st + mp)
                ones = jnp.where((lane >= c0) & (lane < c0 + BIAS_PIECES), 1.0, 0.0)
                qs_ref[st, mp * tile:(mp + 1) * tile, LANES:2 * LANES] = ones.astype(BF16)
    m_ref[...] = jnp.full(m_ref.shape, -jnp.inf, F32)
    l_ref[...] = jnp.zeros_like(l_ref)
    acc_ref[...] = jnp.zeros_like(acc_ref)

    if mode == "fox":
        blk = pl.program_id(1)
        fblk = fq_ref[...]
        for st in range(nstream):
            for mp in range(2):
                head = 2 * (nstream * blk + st) + mp
                fq = jnp.sum(jnp.where(lane == head, fblk, 0.0), axis=1, keepdims=True)
                fqs_ref[st, mp * tile:(mp + 1) * tile, :] = jnp.broadcast_to(fq, (tile, LANES))

    nlb = tile // LANES
    rc = ATT_ROW_CHUNK

    def load_scores(st, ki, r0, masked):
        s = s_ref[st, r0:r0 + rc, :]
        if masked:
            row = lax.broadcasted_iota(jnp.int32, (rc, tile), 0) + (r0 % tile)
            col = lax.broadcasted_iota(jnp.int32, (rc, tile), 1)
            keep = (col // CHUNK) <= (row // CHUNK) if mode == "diff" else col <= row
            s = jnp.where(keep, s, -jnp.inf)
        return s

    def step(ki, masked):
        k0 = pl.multiple_of(ki * tile, tile)
        for st in range(nstream):
            k = k_ref[pl.ds(k0, tile), st * LANES:(st + 1) * LANES]
            if mode == "fox":
                k = jnp.concatenate([k, kb_ref[pl.ds(k0, tile), :]], axis=1)
            s_ref[st] = lax.dot_general(qs_ref[st], k, (((1,), (1,)), ((), ())),
                                        preferred_element_type=F32)
        for st in range(nstream):
            for r0 in range(0, 2 * tile, rc):
                rows = slice(r0, r0 + rc)
                s = load_scores(st, ki, r0, masked)
                mx = s[:, 0:LANES]
                for lb in range(1, nlb):
                    mx = jnp.maximum(mx, s[:, lb * LANES:(lb + 1) * LANES])
                row_max = jnp.broadcast_to(jnp.max(mx, axis=1, keepdims=True), (rc, LANES))
                m_old = m_ref[st, rows, :]
                if mode == "fox":
                    fq = fqs_ref[st, rows, :]
                    m_new = jnp.maximum(m_old, row_max + fq)
                    shift_ref[st, rows, :] = fq - m_new
                else:
                    m_new = jnp.maximum(m_old, row_max)
                    shift_ref[st, rows, :] = -m_new
                alpha_ref[st, rows, :] = jnp.exp2(m_old - m_new)
                m_ref[st, rows, :] = m_new
            for r0 in range(0, 2 * tile, rc):
                rows = slice(r0, r0 + rc)
                s = load_scores(st, ki, r0, masked)
                shift = shift_ref[st, rows, :]
                p = jnp.exp2(s + jnp.concatenate([shift] * nlb, axis=1))
                psum = p[:, 0:LANES]
                for lb in range(1, nlb):
                    psum = psum + p[:, lb * LANES:(lb + 1) * LANES]
                l_ref[st, rows, :] = alpha_ref[st, rows, :] * l_ref[st, rows, :] + psum
                p_ref[st, rows, :] = p.astype(BF16)
            v = v_ref[pl.ds(k0, tile), st * LANES:(st + 1) * LANES]
            acc_ref[st] = alpha_ref[st] * acc_ref[st] + jnp.dot(p_ref[st], v,
                                                                preferred_element_type=F32)

    def body(ki, carry):
        step(ki, False)
        return carry

    lax.fori_loop(0, qi, body, 0)
    step(qi, True)

    if mode == "diff":
        lv = lam_ref[...]
        lam = (jnp.exp(jnp.sum(lv[0:1] * lv[1:2], axis=1, keepdims=True))
               - jnp.exp(jnp.sum(lv[2:3] * lv[3:4], axis=1, keepdims=True)) + lam_init)
    for st in range(nstream):
        o = acc_ref[st] / jnp.sum(l_ref[st], axis=1, keepdims=True)
        o0 = o[0:tile]
        o1 = o[tile:2 * tile]
        cols = slice(st * LANES, (st + 1) * LANES)
        if mode == "diff":
            d = o0 - lam * o1
            d = d * lax.rsqrt(jnp.mean(d * d, axis=-1, keepdims=True) + RMS_EPS)
            o_ref[:, cols] = (d * g_ref[:, cols] * (1.0 - lam_init)).astype(o_ref.dtype)
        else:
            o_ref[:, cols] = jnp.where(low, o0, o1).astype(o_ref.dtype)


def _attention(mode, qkv, batch, seq, nblk, extra, lam_init=0.0):
    tile = ATT_TILE
    nq = seq // tile
    ns = ATT_STREAMS
    width = ns * LANES
    ngrp = nblk // ns
    q_spec = pl.BlockSpec((tile, width), lambda b, p, i: (b * nq + i, p))
    k_spec = pl.BlockSpec((seq, width), lambda b, p, i: (b, ngrp + p))
    v_spec = pl.BlockSpec((seq, width), lambda b, p, i: (b, 2 * ngrp + p))
    stats = pltpu.VMEM((ns, 2 * tile, LANES), F32)
    kdim = 2 * LANES if mode == "fox" else LANES
    scratch = [pltpu.VMEM((ns, 2 * tile, kdim), BF16), stats, stats, stats, stats, stats,
               pltpu.VMEM((ns, 2 * tile, tile), F32),
               pltpu.VMEM((ns, 2 * tile, tile), BF16)]
    if mode == "diff":
        lam_vecs, gains = extra
        extra_specs = [pl.BlockSpec(lam_vecs.shape, lambda b, p, i: (0, 0)),
                       pl.BlockSpec((1, width), lambda b, p, i: (0, p))]
    else:
        extra_specs = [pl.BlockSpec((tile, LANES), lambda b, p, i: (b * nq + i, 0)),
                       pl.BlockSpec((seq, LANES), lambda b, p, i: (b, p))]
        scratch.append(pltpu.VMEM((ns, 2 * tile, LANES), F32))
    return pl.pallas_call(
        functools.partial(_attn_kernel, mode=mode, tile=tile, lam_init=lam_init),
        grid=(batch, ngrp, nq),
        in_specs=[q_spec, k_spec, v_spec] + extra_specs,
        out_specs=pl.BlockSpec((tile, width), lambda b, p, i: (b * nq + i, p)),
        out_shape=jax.ShapeDtypeStruct((batch * seq, nblk * LANES), BF16),
        scratch_shapes=scratch,
        compiler_params=_params("parallel", "parallel", "parallel"),
        name=mode + "_attention",
    )(qkv, qkv, qkv, *extra)


def _s5_kernel(u_ref, bh_ref, ch_ref, pw_ref, d_ref, gw_ref, gb_ref, o_ref,
               up_ref, bu_ref, xb_ref, yp_ref, carry_ref):
    seg, nseg = SSM_SEG, SUBLANES
    half = bh_ref.shape[1]
    nst = bh_ref.shape[2]
    ncx = nst // 2

    @pl.when(pl.program_id(1) == 0)
    def _():
        carry_ref[...] = jnp.zeros_like(carry_ref)

    nlb = up_ref.shape[0]
    u = u_ref[...].astype(F32)
    for lb in range(nlb):
        for s in range(nseg):
            up_ref[lb, pl.ds(s, seg, stride=nseg), :] = (
                u[s * seg:(s + 1) * seg, lb * LANES:(lb + 1) * LANES])
    ub = jnp.concatenate([up_ref[lb] for lb in range(nlb)], axis=1).astype(BF16)
    for hf in range(2):
        bu_ref[:, hf * nst:(hf + 1) * nst] = jnp.dot(
            ub[:, hf * half:(hf + 1) * half], bh_ref[hf], preferred_element_type=F32)

    sub = lax.broadcasted_iota(jnp.int32, (nseg, SSM_COLS), 0)
    for hf in range(2):
        for c0 in range(0, ncx, SSM_COLS):
            re = slice(hf * nst + c0, hf * nst + c0 + SSM_COLS)
            im = slice(hf * nst + ncx + c0, hf * nst + ncx + c0 + SSM_COLS)
            ar = jnp.broadcast_to(pw_ref[0:1, re], (nseg, SSM_COLS))
            ai = jnp.broadcast_to(pw_ref[0:1, im], (nseg, SSM_COLS))

            def scan_step(t, x, re=re, im=im, ar=ar, ai=ai):
                xr, xi = x
                rows = pl.ds(pl.multiple_of(t * nseg, nseg), nseg)
                nr = ar * xr - ai * xi + bu_ref[rows, re]
                ni = ar * xi + ai * xr + bu_ref[rows, im]
                bu_ref[rows, re] = nr
                bu_ref[rows, im] = ni
                return nr, ni

            zeros = jnp.zeros((nseg, SSM_COLS), F32)
            er, ei = lax.fori_loop(0, seg, scan_step, (zeros, zeros))

            a_seg_r = pw_ref[seg - 1:seg, re]
            a_seg_i = pw_ref[seg - 1:seg, im]
            cr = carry_ref[0:1, re]
            ci = carry_ref[0:1, im]
            cmr = jnp.zeros((nseg, SSM_COLS), F32)
            cmi = jnp.zeros((nseg, SSM_COLS), F32)
            for s in range(nseg):
                cmr = jnp.where(sub == s, jnp.broadcast_to(cr, cmr.shape), cmr)
                cmi = jnp.where(sub == s, jnp.broadcast_to(ci, cmi.shape), cmi)
                nr = a_seg_r * cr - a_seg_i * ci + er[s:s + 1, :]
                ni = a_seg_r * ci + a_seg_i * cr + ei[s:s + 1, :]
                cr, ci = nr, ni
            carry_ref[:, re] = jnp.broadcast_to(cr, (nseg, SSM_COLS))
            carry_ref[:, im] = jnp.broadcast_to(ci, (nseg, SSM_COLS))

            def fix_step(t, carry, re=re, im=im, cmr=cmr, cmi=cmi):
                rows = pl.ds(pl.multiple_of(t * nseg, nseg), nseg)
                pr = pw_ref[pl.ds(t, 1), re]
                pi = pw_ref[pl.ds(t, 1), im]
                xr = bu_ref[rows, re] + (pr * cmr - pi * cmi)
                xi = bu_ref[rows, im] + (pr * cmi + pi * cmr)
                bu_ref[rows, re] = xr
                bu_ref[rows, im] = xi
                return carry

            lax.fori_loop(0, seg, fix_step, 0)

    xb_ref[...] = bu_ref[...].astype(BF16)
    for hf in range(2):
        yh = jnp.dot(xb_ref[:, hf * nst:(hf + 1) * nst], ch_ref[hf],
                     preferred_element_type=F32)
        for lb in range(half // LANES):
            yp_ref[hf * (half // LANES) + lb] = yh[:, lb * LANES:(lb + 1) * LANES]
    y = jnp.concatenate(
        [jnp.concatenate([yp_ref[lb, pl.ds(s, seg, stride=nseg), :] for s in range(nseg)], axis=0)
         for lb in range(nlb)], axis=1)
    y = y + d_ref[...] * u_ref[...].astype(F32)
    y = jax.nn.gelu(y)
    gate = jnp.dot(y.astype(BF16), gw_ref[...], preferred_element_type=F32) + gb_ref[...]
    o_ref[...] = (y * jax.nn.sigmoid(gate)).astype(o_ref.dtype)


def _s5(proj, u_block, bh, ch, pw, d_skip, glu_w, glu_b, batch, seq):
    width = bh.shape[0] * bh.shape[1]
    nstate = bh.shape[0] * bh.shape[2]
    rows = SSM_ROWS
    nc = seq // rows
    const2 = lambda b, i: (0, 0)
    const3 = lambda b, i: (0, 0, 0)
    return pl.pallas_call(
        _s5_kernel,
        grid=(batch, nc),
        in_specs=[pl.BlockSpec((rows, width), lambda b, i: (b * nc + i, u_block)),
                  pl.BlockSpec(bh.shape, const3),
                  pl.BlockSpec(ch.shape, const3),
                  pl.BlockSpec(pw.shape, const2),
                  pl.BlockSpec((1, width), const2),
                  pl.BlockSpec(glu_w.shape, const2),
                  pl.BlockSpec((1, width), const2)],
        out_specs=pl.BlockSpec((rows, width), lambda b, i: (b * nc + i, 0)),
        out_shape=jax.ShapeDtypeStruct((batch * seq, width), BF16),
        scratch_shapes=[pltpu.VMEM((width // LANES, rows, LANES), F32),
                        pltpu.VMEM((rows, nstate), F32),
                        pltpu.VMEM((rows, nstate), BF16),
                        pltpu.VMEM((width // LANES, rows, LANES), F32),
                        pltpu.VMEM((SUBLANES, nstate), F32)],
        compiler_params=_params("parallel", "arbitrary"),
        name="s5_mixer",
    )(proj, bh, ch, pw, d_skip, glu_w, glu_b)


def _s5_tables(a_re, a_im, log_dt, b_re, b_im, c_re, c_im):
    groups, nstate = a_re.shape
    gh = groups // 2
    dt = jnp.exp(log_dt.astype(F32))[:, None]
    a_re = a_re.astype(F32)
    a_im = a_im.astype(F32)
    mag = jnp.exp(a_re * dt)
    ab_re = mag * jnp.cos(a_im * dt)
    ab_im = mag * jnp.sin(a_im * dt)
    den = jnp.square(a_re) + jnp.square(a_im)
    nr = ab_re - 1.0
    ni = ab_im
    g_re = (nr * a_re + ni * a_im) / den
    g_im = (ni * a_re - nr * a_im) / den
    b_re = b_re.astype(F32)
    b_im = b_im.astype(F32)
    bb_re = g_re[..., None] * b_re - g_im[..., None] * b_im
    bb_im = g_re[..., None] * b_im + g_im[..., None] * b_re
    eye = jnp.eye(gh, dtype=F32)

    def in_map(bb):
        return jnp.einsum('gph,gk->ghkp', bb, eye).reshape(gh * SSM_GROUP, gh * nstate)

    def out_map(c):
        return jnp.einsum('ghp,gk->gpkh', c, eye).reshape(gh * nstate, gh * SSM_GROUP)

    bh = jnp.stack([jnp.concatenate([in_map(bb_re[h * gh:(h + 1) * gh]),
                                     in_map(bb_im[h * gh:(h + 1) * gh])], axis=1)
                    for h in range(2)]).astype(BF16)
    ch = jnp.stack([jnp.concatenate([out_map(c_re[h * gh:(h + 1) * gh].astype(F32)),
                                     -out_map(c_im[h * gh:(h + 1) * gh].astype(F32))], axis=0)
                    for h in range(2)]).astype(BF16)
    def pw_step(c, _):
        pr, pi = c
        nxt = (pr * ab_re - pi * ab_im, pr * ab_im + pi * ab_re)
        return nxt, c
    _, (pw_re, pw_im) = lax.scan(pw_step, (ab_re, ab_im), None, length=SSM_SEG)
    pw = jnp.concatenate(
        [jnp.concatenate([pw_re[:, h * gh:(h + 1) * gh].reshape(SSM_SEG, gh * nstate),
                          pw_im[:, h * gh:(h + 1) * gh].reshape(SSM_SEG, gh * nstate)], axis=1)
         for h in range(2)], axis=1)
    return bh, ch, pw


def _outproj_ln_kernel(*refs, n_in, alpha):
    a_refs = refs[:n_in]
    w_refs = refs[n_in:2 * n_in]
    x_ref, g_ref, b_ref, o_ref = refs[2 * n_in:]
    acc = alpha * x_ref[...]
    for a_ref, w_ref in zip(a_refs, w_refs):
        acc = acc + jnp.dot(a_ref[...], w_ref[...], preferred_element_type=F32)
    o_ref[...] = _layer_norm(acc, g_ref[...], b_ref[...])


def _outproj_ln(acts, weights, x2, g, b, alpha):
    t, d = x2.shape
    tm = ROW_TILE
    n_in = len(acts)
    in_specs = ([pl.BlockSpec((tm, a.shape[1]), lambda i: (i, 0)) for a in acts]
                + [pl.BlockSpec(w.shape, lambda i: (0, 0)) for w in weights]
                + [pl.BlockSpec((tm, d), lambda i: (i, 0)),
                   pl.BlockSpec((1, d), lambda i: (0, 0)),
                   pl.BlockSpec((1, d), lambda i: (0, 0))])
    return pl.pallas_call(
        functools.partial(_outproj_ln_kernel, n_in=n_in, alpha=alpha),
        grid=(t // tm,),
        in_specs=in_specs,
        out_specs=pl.BlockSpec((tm, d), lambda i: (i, 0)),
        out_shape=jax.ShapeDtypeStruct((t, d), F32),
        compiler_params=_params("parallel"),
        name="outproj_ln",
    )(*acts, *weights, x2, g, b)


def _ffn_kernel(h_ref, w1_ref, w2_ref, g_ref, b_ref, o_ref, hb_ref, acc_ref, *, alpha):
    j = pl.program_id(1)

    @pl.when(j == 0)
    def _():
        hb_ref[...] = h_ref[...].astype(BF16)
        acc_ref[...] = jnp.zeros_like(acc_ref)

    a = jnp.dot(hb_ref[...], w1_ref[...], preferred_element_type=F32)
    a = jnp.square(jnp.maximum(a, 0.0)).astype(BF16)
    acc_ref[...] += jnp.dot(a, w2_ref[...], preferred_element_type=F32)

    @pl.when(j == pl.num_programs(1) - 1)
    def _():
        o_ref[...] = _layer_norm(alpha * h_ref[...] + acc_ref[...], g_ref[...], b_ref[...])


def _ffn(h2, w1, w2, g, b, alpha):
    t, d = h2.shape
    dff = w1.shape[1]
    tm, tf = FFN_ROW_TILE, FFN_COL_TILE
    return pl.pallas_call(
        functools.partial(_ffn_kernel, alpha=alpha),
        grid=(t // tm, dff // tf),
        in_specs=[pl.BlockSpec((tm, d), lambda i, j: (i, 0)),
                  pl.BlockSpec((d, tf), lambda i, j: (0, j)),
                  pl.BlockSpec((tf, d), lambda i, j: (j, 0)),
                  pl.BlockSpec((1, d), lambda i, j: (0, 0)),
                  pl.BlockSpec((1, d), lambda i, j: (0, 0))],
        out_specs=pl.BlockSpec((tm, d), lambda i, j: (i, 0)),
        out_shape=jax.ShapeDtypeStruct((t, d), F32),
        scratch_shapes=[pltpu.VMEM((tm, d), BF16), pltpu.VMEM((tm, d), F32)],
        compiler_params=_params("parallel", "arbitrary"),
        name="ffn_ln",
    )(h2, w1, w2, g, b)


def _rope_tables(seq):
    inv = ROPE_THETA ** (-jnp.arange(0, QK_DIM, 2, dtype=F32) / QK_DIM)
    ang = jnp.arange(seq, dtype=F32)[:, None] * inv[None, :]
    cos, sin = jnp.cos(ang), jnp.sin(ang)
    reps = LANES // QK_DIM
    cos128 = jnp.tile(cos, (1, 2 * reps))
    sin128 = jnp.tile(jnp.concatenate([-sin, sin], axis=1), (1, reps))
    return cos128, sin128


def kernel(x, ab_w_in, ab_lambda, ab_subln_g, ssm_a_re, ssm_a_im, ssm_log_dt, ssm_b_re, ssm_b_im, ssm_c_re, ssm_c_im, ssm_d, ssm_glu_w, ssm_glu_b, ab_w_out, fox_w_in, fox_b_f, fox_w_out, ln1_g, ln1_b, ffn_w1, ffn_w2, ln2_g, ln2_b):
    batch, seq, d_model = x.shape
    depth = ln1_g.shape[0]
    alpha = (2 * depth) ** 0.25
    assert seq % SSM_ROWS == 0 and seq % ROW_TILE == 0 and (batch * seq) % FFN_ROW_TILE == 0

    diff_heads, diff_v = ab_subln_g.shape[1], ab_subln_g.shape[2]
    assert diff_v == LANES == 2 * QK_DIM
    diff_width = diff_heads * diff_v
    ssm_width = ssm_d.shape[1]
    fox_heads = fox_b_f.shape[1]
    assert d_model == fox_heads * QK_DIM and ssm_width == 2 * 16 * SSM_GROUP

    cos128, sin128 = _rope_tables(seq)
    h = x.reshape(batch * seq, d_model)
    for l in range(depth):
        i = l // 2
        if l % 2 == 0:
            lam_init = 0.8 - 0.6 * math.exp(-0.3 * l)
            proj = _proj_even(h, ab_w_in[i].astype(BF16), cos128, sin128, seq, diff_width)
            attn = _attention("diff", proj, batch, seq, diff_heads,
                              (ab_lambda[i].astype(F32),
                               ab_subln_g[i].astype(F32).reshape(1, diff_width)),
                              lam_init=lam_init)
            bh, ch, pw = _s5_tables(ssm_a_re[i], ssm_a_im[i], ssm_log_dt[i], ssm_b_re[i],
                                    ssm_b_im[i], ssm_c_re[i], ssm_c_im[i])
            y_ssm = _s5(proj, 3 * diff_width // ssm_width, bh, ch, pw,
                        ssm_d[i].astype(F32).reshape(1, ssm_width),
                        ssm_glu_w[i].astype(BF16),
                        ssm_glu_b[i].astype(F32).reshape(1, ssm_width), batch, seq)
            w_out = ab_w_out[i].astype(BF16)
            acts = [attn, y_ssm]
            weights = [w_out[:diff_width], w_out[diff_width:]]
        else:
            w_in = fox_w_in[i]
            w_pad = jnp.pad(w_in, ((0, 0), (0, LANES - fox_heads))).astype(BF16)
            qkv, flog = _proj_odd(h, w_pad, d_model)
            b_pad = jnp.pad(fox_b_f[i].astype(F32), (0, LANES - fox_heads)).reshape(1, LANES)
            sel = _bias_routing(fox_heads, 2 * ATT_STREAMS)
            fcum, key_bias = _fgate(flog, b_pad, sel, batch, seq)
            attn = _attention("fox", qkv, batch, seq, fox_heads // 2, (fcum, key_bias))
            acts = [attn]
            weights = [fox_w_out[i].astype(BF16)]
        h = _outproj_ln(acts, weights, h, ln1_g[l].reshape(1, d_model),
                        ln1_b[l].reshape(1, d_model), alpha)
        h = _ffn(h, ffn_w1[l].astype(BF16), ffn_w2[l].astype(BF16),
                 ln2_g[l].reshape(1, d_model), ln2_b[l].reshape(1, d_model), alpha)
    return h.reshape(batch, seq, d_model)
```
